```python
import functools
import jax, jax.numpy as jnp
from jax import lax
import numpy as np

D_MODEL = 1024
BATCH = 4
SEQ = 4096
DEPTH = 1
DEC_BATCH = 128
DEC_SEQ = 4
PAST_LEN = 8192
PAGE_SIZE = 128

SSM_WIDTH = D_MODEL
SSM_HEADDIM = 64
SSM_HEADS = SSM_WIDTH // SSM_HEADDIM
SSM_GROUPS = 4
SSM_STATE = 128
CONV_WIDTH = 4
SSD_CHUNK = 128
XBC_WIDTH = SSM_WIDTH + 2 * SSM_GROUPS * SSM_STATE
ATTN_HEADDIM = 128
ATTN_HEADS = D_MODEL // ATTN_HEADDIM
KV_HEADS = ATTN_HEADS // 2
ATTN_WIDTH = ATTN_HEADS * ATTN_HEADDIM
KV_WIDTH = KV_HEADS * ATTN_HEADDIM
MOBA_BLOCK = 256
MOBA_TOPK = 3
Q_BLOCK = 128
ROT_DIM = ATTN_HEADDIM // 4
ROPE_THETA = 500000.0
MIX_WIDTH = SSM_WIDTH + ATTN_WIDTH
IN_SIZES = (SSM_WIDTH, XBC_WIDTH, SSM_HEADS, ATTN_WIDTH, KV_WIDTH, KV_WIDTH, ATTN_WIDTH)
IN_WIDTH = sum(IN_SIZES)
EPS = 1e-6
NEG = -1e30

kernel_name = "hymba_ssd_moba_decode_step"


def _rmsnorm(x, w):
    xf = x.astype(jnp.float32)
    r = lax.rsqrt(jnp.mean(xf * xf, axis=-1, keepdims=True) + EPS)
    return (xf * r * w.astype(jnp.float32)).astype(x.dtype)


def _rope(x, pos):
    half = ROT_DIM // 2
    inv = 1.0 / (ROPE_THETA ** (jnp.arange(0, ROT_DIM, 2, dtype=jnp.float32) / ROT_DIM))
    ang = pos.astype(jnp.float32)[:, None] * inv[None, :]
    cos = jnp.cos(ang)[None, :, None, :]
    sin = jnp.sin(ang)[None, :, None, :]
    xf = x.astype(jnp.float32)
    x1, x2 = xf[..., :half], xf[..., half:ROT_DIM]
    out = jnp.concatenate([x1 * cos - x2 * sin, x2 * cos + x1 * sin, xf[..., ROT_DIM:]], axis=-1)
    return out.astype(x.dtype)


def _ssd(xh, dt, A, Bm, Cm, h0, d_skip):
    f32 = jnp.float32
    b, L, H, P = xh.shape
    G, N = Bm.shape[2], Bm.shape[3]
    r = H // G
    l = SSD_CHUNK if L % SSD_CHUNK == 0 else L
    c = L // l
    x = xh.astype(f32).reshape(b, c, l, G, r, P)
    dtc = dt.reshape(b, c, l, G, r)
    xdt = x * dtc[..., None]
    acum = jnp.cumsum(dtc * A.reshape(G, r), axis=2)
    Bc = Bm.astype(f32).reshape(b, c, l, G, N)
    Cc = Cm.astype(f32).reshape(b, c, l, G, N)
    seg = acum[:, :, :, None] - acum[:, :, None, :]
    causal = jnp.tril(jnp.ones((l, l), dtype=bool))[None, None, :, :, None, None]
    lmat = jnp.where(causal, jnp.exp(jnp.where(causal, seg, 0.0)), 0.0)
    cb = jnp.einsum('bcign,bcjgn->bcijg', Cc, Bc)
    y_diag = jnp.einsum('bcijg,bcijgr,bcjgrp->bcigrp', cb, lmat, xdt)
    decay_end = jnp.exp(acum[:, :, -1:] - acum)
    states = jnp.einsum('bcjgn,bcjgr,bcjgrp->bcgrpn', Bc, decay_end, xdt)
    chunk_decay = jnp.exp(acum[:, :, -1])

    def step(hc, inp):
        st, dec = inp
        return hc * dec[..., None, None] + st, hc

    h_final, h_starts = lax.scan(step, h0.astype(f32).reshape(b, G, r, P, N),
                                 (jnp.moveaxis(states, 1, 0), jnp.moveaxis(chunk_decay, 1, 0)))
    h_starts = jnp.moveaxis(h_starts, 0, 1)
    y_off = jnp.einsum('bcign,bcgrpn,bcigr->bcigrp', Cc, h_starts, jnp.exp(acum))
    y = (y_diag + y_off).reshape(b, L, H, P) + d_skip.astype(f32)[:, None] * xh.astype(f32)
    return y.astype(xh.dtype), h_final.reshape(b, H, P, N)


def _moba_attend(qs, ko, vo, own_mask, ksel, vsel, sel_valid):
    f32 = jnp.float32
    scale = qs.shape[-1] ** -0.5
    s_own = jnp.einsum('bqgrd,bkgd->bqgrk', qs, ko, preferred_element_type=f32) * scale
    s_own = jnp.where(own_mask[None, :, None, None, :], s_own, NEG)
    if ksel is None:
        p = jax.nn.softmax(s_own, axis=-1).astype(vo.dtype)
        o = jnp.einsum('bqgrk,bkgd->bqgrd', p, vo, preferred_element_type=f32)
        return o.astype(qs.dtype)
    s_sel = jnp.einsum('bqgrd,bqgrsnd->bqgrsn', qs, ksel, preferred_element_type=f32) * scale
    if sel_valid is not None:
        s_sel = jnp.where(sel_valid[:, None], s_sel, NEG)
    n_sel = s_sel.shape[-2] * s_sel.shape[-1]
    s = jnp.concatenate([s_sel.reshape(s_sel.shape[:-2] + (n_sel,)), s_own], axis=-1)
    p = jax.nn.softmax(s, axis=-1).astype(vo.dtype)
    o = (jnp.einsum('bqgrsn,bqgrsnd->bqgrd', p[..., :n_sel].reshape(s_sel.shape), vsel,
                    preferred_element_type=f32)
         + jnp.einsum('bqgrk,bkgd->bqgrd', p[..., n_sel:], vo, preferred_element_type=f32))
    return o.astype(qs.dtype)


def _moba_prompt(q, k, v):
    b, L, H, d = q.shape
    R = H // KV_HEADS
    nblk = -(-L // MOBA_BLOCK)
    pad = nblk * MOBA_BLOCK - L
    kp = jnp.pad(k, ((0, 0), (0, pad), (0, 0), (0, 0)))
    vp = jnp.pad(v, ((0, 0), (0, pad), (0, 0), (0, 0)))
    kb = kp.reshape(b, nblk, MOBA_BLOCK, KV_HEADS, d)
    vb = vp.reshape(b, nblk, MOBA_BLOCK, KV_HEADS, d)
    kmean = jnp.mean(kb.astype(jnp.float32), axis=2).astype(k.dtype)
    kbt = kb.transpose(0, 3, 1, 2, 4)
    vbt = vb.transpose(0, 3, 1, 2, 4)
    k_sel = min(MOBA_TOPK, nblk - 1)
    qg = q.reshape(b, L, KV_HEADS, R, d)
    bi = jnp.arange(b)[:, None, None, None, None]
    gi = jnp.arange(KV_HEADS)[None, None, :, None, None]

    def one_block(n):
        start = n * Q_BLOCK
        qs = lax.dynamic_slice_in_dim(qg, start, Q_BLOCK, axis=1)
        own = start // MOBA_BLOCK
        ko = lax.dynamic_slice_in_dim(kp, own * MOBA_BLOCK, MOBA_BLOCK, axis=1)
        vo = lax.dynamic_slice_in_dim(vp, own * MOBA_BLOCK, MOBA_BLOCK, axis=1)
        qpos = start + jnp.arange(Q_BLOCK)
        kpos = own * MOBA_BLOCK + jnp.arange(MOBA_BLOCK)
        own_mask = kpos[None, :] <= qpos[:, None]
        if k_sel == 0:
            return _moba_attend(qs, ko, vo, own_mask, None, None, None)
        gate = jnp.einsum('bqgrd,bngd->bqgrn', qs, kmean, preferred_element_type=jnp.float32)
        gate = jnp.where(jnp.arange(nblk) < own, gate, NEG)
        _, idx = lax.top_k(gate, k_sel)
        ksel = kbt[bi, gi, idx]
        vsel = vbt[bi, gi, idx]
        sel_valid = jnp.arange(k_sel) < own
        return _moba_attend(qs, ko, vo, own_mask, ksel, vsel, sel_valid)

    o = lax.map(one_block, jnp.arange(L // Q_BLOCK))
    return o.transpose(1, 0, 2, 3, 4, 5).reshape(b, L, H * d)


def _moba_sample(q, k_new, v_new, cache_k, cache_v, page_table):
    b, S, H, d = q.shape
    R = H // KV_HEADS
    past = page_table.shape[1] * PAGE_SIZE
    ppb = MOBA_BLOCK // PAGE_SIZE
    n_full = past // MOBA_BLOCK
    own_start = n_full * MOBA_BLOCK
    own_pages = (past - own_start) // PAGE_SIZE
    qg = q.reshape(b, S, KV_HEADS, R, d)
    own_pt = page_table[:, n_full * ppb:n_full * ppb + own_pages]
    ko = jnp.concatenate([cache_k[own_pt].reshape(b, own_pages * PAGE_SIZE, KV_HEADS, d), k_new], axis=1)
    vo = jnp.concatenate([cache_v[own_pt].reshape(b, own_pages * PAGE_SIZE, KV_HEADS, d), v_new], axis=1)
    qpos = past + jnp.arange(S)
    kpos = own_start + jnp.arange(own_pages * PAGE_SIZE + S)
    own_mask = kpos[None, :] <= qpos[:, None]
    k_sel = min(MOBA_TOPK, n_full)
    if k_sel == 0:
        o = _moba_attend(qg, ko, vo, own_mask, None, None, None)
        return o.reshape(b, S, H * d)
    full_pt = page_table[:, :n_full * ppb]
    kmean = jnp.mean(cache_k[full_pt].astype(jnp.float32)
                     .reshape(b, n_full, ppb * PAGE_SIZE, KV_HEADS, d), axis=2).astype(q.dtype)
    gate = jnp.einsum('bqgrd,bngd->bqgrn', qg, kmean, preferred_element_type=jnp.float32)
    _, idx = lax.top_k(gate, k_sel)
    logical = idx[..., None] * ppb + jnp.arange(ppb)
    bi6 = jnp.arange(b)[:, None, None, None, None, None]
    gi6 = jnp.arange(KV_HEADS)[None, None, :, None, None, None]
    phys = page_table[bi6, logical]
    ksel = cache_k[phys, :, gi6].reshape(b, S, KV_HEADS, R, k_sel, ppb * PAGE_SIZE, d)
    vsel = cache_v[phys, :, gi6].reshape(b, S, KV_HEADS, R, k_sel, ppb * PAGE_SIZE, d)
    o = _moba_attend(qg, ko, vo, own_mask, ksel, vsel, None)
    return o.reshape(b, S, H * d)


def _layer(x, pos0, conv_prefix, h0, attn_fn, norm_w, w_in, conv_w, conv_b,
           dt_bias, a_log, d_skip, ssm_norm_w, w_out):
    b, L, _ = x.shape
    h = _rmsnorm(x, norm_w)
    proj = h @ w_in
    cuts = [int(c) for c in np.cumsum(IN_SIZES)[:-1]]
    z, xbc, dt_raw, q, k, v, g = jnp.split(proj, cuts, axis=-1)
    xpad = jnp.concatenate([conv_prefix.astype(xbc.dtype), xbc], axis=1)
    conv = conv_b + sum(xpad[:, i:i + L] * conv_w[i] for i in range(CONV_WIDTH))
    new_conv = xpad[:, -(CONV_WIDTH - 1):]
    xbc_c = jax.nn.silu(conv)
    xs, Bm, Cm = jnp.split(xbc_c, [SSM_WIDTH, SSM_WIDTH + SSM_GROUPS * SSM_STATE], axis=-1)
    dt = jax.nn.softplus(dt_raw.astype(jnp.float32) + dt_bias.astype(jnp.float32))
    A = -jnp.exp(a_log.astype(jnp.float32))
    y, h_new = _ssd(xs.reshape(b, L, SSM_HEADS, SSM_HEADDIM), dt, A,
                    Bm.reshape(b, L, SSM_GROUPS, SSM_STATE), Cm.reshape(b, L, SSM_GROUPS, SSM_STATE),
                    h0, d_skip)
    y_ssm = _rmsnorm(y.reshape(b, L, SSM_WIDTH) * jax.nn.silu(z), ssm_norm_w)
    pos = pos0 + jnp.arange(L)
    q = _rope(q.reshape(b, L, ATTN_HEADS, ATTN_HEADDIM), pos)
    k = _rope(k.reshape(b, L, KV_HEADS, ATTN_HEADDIM), pos)
    v = v.reshape(b, L, KV_HEADS, ATTN_HEADDIM)
    y_attn = attn_fn(q, k, v) * jax.nn.silu(g)
    out = jnp.concatenate([y_ssm, y_attn], axis=-1) @ w_out
    return x + out, k, v, h_new.astype(h0.dtype), new_conv


def setup_inputs(seed: int = 0) -> dict:
    key = jax.random.key(seed)
    ks = jax.random.split(key, 20)
    f32 = jnp.float32
    n_pages = PAST_LEN // PAGE_SIZE
    used = DEC_BATCH * n_pages
    n_pool = used + max(1, used // 4)
    page_table = jax.random.permutation(ks[0], n_pool)[:used].reshape(DEC_BATCH, n_pages).astype(jnp.int32)
    dt0 = jnp.exp(jax.random.uniform(ks[1], (DEPTH, SSM_HEADS), f32, np.log(1e-3), np.log(1e-1)))
    return {
        "x_prompt": jax.random.normal(ks[2], (BATCH, SEQ, D_MODEL), f32),
        "x_sample": jax.random.normal(ks[3], (DEC_BATCH, DEC_SEQ, D_MODEL), f32),
        "cache_k": jax.random.normal(ks[4], (DEPTH, n_pool, PAGE_SIZE, KV_HEADS, ATTN_HEADDIM), f32),
        "cache_v": jax.random.normal(ks[5], (DEPTH, n_pool, PAGE_SIZE, KV_HEADS, ATTN_HEADDIM), f32),
        "state_ssm": 0.5 * jax.random.normal(ks[6], (DEPTH, DEC_BATCH, SSM_HEADS, SSM_HEADDIM, SSM_STATE), f32),
        "state_conv": jax.random.normal(ks[7], (DEPTH, DEC_BATCH, CONV_WIDTH - 1, XBC_WIDTH), f32),
        "page_table": page_table,
        "norm_w": 1.0 + 0.02 * jax.random.normal(ks[8], (DEPTH, D_MODEL), f32),
        "w_in": jax.random.normal(ks[9], (DEPTH, D_MODEL, IN_WIDTH), f32) * D_MODEL ** -0.5,
        "conv_w": jax.random.normal(ks[10], (DEPTH, CONV_WIDTH, XBC_WIDTH), f32) * CONV_WIDTH ** -0.5,
        "conv_b": 0.01 * jax.random.normal(ks[11], (DEPTH, XBC_WIDTH), f32),
        "dt_bias": dt0 + jnp.log(-jnp.expm1(-dt0)),
        "a_log": jnp.log(jax.random.uniform(ks[12], (DEPTH, SSM_HEADS), f32, 1.0, 16.0)),
        "d_skip": 1.0 + 0.1 * jax.random.normal(ks[13], (DEPTH, SSM_HEADS), f32),
        "ssm_norm_w": 1.0 + 0.02 * jax.random.normal(ks[14], (DEPTH, SSM_WIDTH), f32),
        "w_out": jax.random.normal(ks[15], (DEPTH, MIX_WIDTH, D_MODEL), f32) * MIX_WIDTH ** -0.5,
        "final_norm_w": 1.0 + 0.02 * jax.random.normal(ks[16], (D_MODEL,), f32),
    }


def reference(x_prompt, x_sample, cache_k, cache_v, state_ssm, state_conv, page_table,
              norm_w, w_in, conv_w, conv_b, dt_bias, a_log, d_skip, ssm_norm_w, w_out, final_norm_w):
    past = page_table.shape[1] * PAGE_SIZE
    xp, xs = x_prompt, x_sample
    bp = x_prompt.shape[0]
    kp_l, vp_l, hp_l, cp_l, ks_l, vs_l, hs_l, cs_l = [], [], [], [], [], [], [], []
    for l in range(DEPTH):
        params = (norm_w[l], w_in[l], conv_w[l], conv_b[l], dt_bias[l], a_log[l],
                  d_skip[l], ssm_norm_w[l], w_out[l])
        conv0 = jnp.zeros((bp, CONV_WIDTH - 1, XBC_WIDTH), x_prompt.dtype)
        h00 = jnp.zeros((bp, SSM_HEADS, SSM_HEADDIM, SSM_STATE), state_ssm.dtype)
        xp, kp, vp, hp, cp = _layer(xp, 0, conv0, h00, _moba_prompt, *params)
        attn_s = functools.partial(_moba_sample, cache_k=cache_k[l], cache_v=cache_v[l], page_table=page_table)
        xs, ks_, vs_, hs, cs = _layer(xs, past, state_conv[l], state_ssm[l], attn_s, *params)
        kp_l.append(kp); vp_l.append(vp); hp_l.append(hp); cp_l.append(cp)
        ks_l.append(ks_); vs_l.append(vs_); hs_l.append(hs); cs_l.append(cs)
    y_prompt = _rmsnorm(xp, final_norm_w)
    y_sample = _rmsnorm(xs, final_norm_w)
    return (y_prompt, y_sample,
            jnp.stack(kp_l), jnp.stack(vp_l), jnp.stack(ks_l), jnp.stack(vs_l),
            jnp.stack(hp_l), jnp.stack(cp_l), jnp.stack(hs_l), jnp.stack(cs_l))
```

```python
import functools

import jax
import jax.numpy as jnp
from jax import lax
from jax.experimental import pallas as pl
from jax.experimental.pallas import tpu as pltpu

F32 = jnp.float32
BF16 = jnp.bfloat16

D_MODEL = 1024
SSM_WIDTH = 1024
SSM_HEADDIM = 64
SSM_HEADS = SSM_WIDTH // SSM_HEADDIM
SSM_GROUPS = 4
HEADS_PER_GROUP = SSM_HEADS // SSM_GROUPS
SSM_STATE = 128
CONV_WIDTH = 4
SSD_CHUNK = 128
XBC_WIDTH = SSM_WIDTH + 2 * SSM_GROUPS * SSM_STATE
ATTN_HEADDIM = 128
ATTN_HEADS = 8
KV_HEADS = 4
Q_PER_KV = ATTN_HEADS // KV_HEADS
ATTN_WIDTH = ATTN_HEADS * ATTN_HEADDIM
KV_WIDTH = KV_HEADS * ATTN_HEADDIM
MOBA_BLOCK = 256
MOBA_TOPK = 3
Q_BLOCK = 128
ROT_DIM = 32
ROPE_THETA = 500000.0
PAGE_SIZE = 128
PAGES_PER_BLOCK = MOBA_BLOCK // PAGE_SIZE
EPS = 1e-6
NEG = -1e30
LANES = 128
DT_PAD = LANES
PROJ_WIDTH = SSM_WIDTH + XBC_WIDTH + ATTN_WIDTH + 2 * KV_WIDTH + ATTN_WIDTH + DT_PAD
VMEM_LIMIT = 56 * 1024 * 1024
SAMPLE_TILE = 8


def _nt(a, b):
    return lax.dot_general(a, b, (((1,), (1,)), ((), ())), preferred_element_type=F32)


def _tn(a, b):
    return lax.dot_general(a, b, (((0,), (0,)), ((), ())), preferred_element_type=F32)


def _nn(a, b):
    return jnp.dot(a, b, preferred_element_type=F32)


def _split3(x):
    hi = x.astype(BF16)
    r1 = x - hi.astype(F32)
    mid = r1.astype(BF16)
    lo = (r1 - mid.astype(F32)).astype(BF16)
    return hi, mid, lo


def _nn_exact_rhs(x, e):
    hi, mid, lo = _split3(x)
    return _nn(hi, e) + _nn(mid, e) + _nn(lo, e)


def _nt_x3(a, b):
    ah, am, _ = _split3(a)
    bh, bm, _ = _split3(b)
    return _nt(ah, bh) + _nt(ah, bm) + _nt(am, bh)


def _silu(x):
    return x / (1.0 + jnp.exp(-x))


def _softplus(x):
    return jnp.maximum(x, 0.0) + jnp.log1p(jnp.exp(-jnp.abs(x)))


def _rmsnorm(x, w):
    ms = jnp.mean(x * x, axis=-1, keepdims=True)
    return x * lax.rsqrt(ms + EPS) * w


def _proj_kernel(x_ref, nw_ref, w_ref, cos_ref, sin_ref,
                 z_ref, xbc_ref, q_ref, k_ref, v_ref, g_ref, dt_ref):
    h = _rmsnorm(x_ref[...], nw_ref[...]).astype(BF16)

    def mm(a, b):
        return _nn(h, w_ref[:, a:b])

    o = 0
    z_ref[...] = mm(o, o + SSM_WIDTH); o += SSM_WIDTH
    xbc_ref[...] = mm(o, o + XBC_WIDTH); o += XBC_WIDTH
    q = mm(o, o + ATTN_WIDTH); o += ATTN_WIDTH
    k = mm(o, o + KV_WIDTH); o += KV_WIDTH
    v_ref[...] = mm(o, o + KV_WIDTH); o += KV_WIDTH
    g_ref[...] = mm(o, o + ATTN_WIDTH); o += ATTN_WIDTH
    dt_ref[...] = mm(o, o + DT_PAD)

    cosf = cos_ref[...]
    sinf = sin_ref[...]
    first_half = lax.broadcasted_iota(jnp.int32, cosf.shape, 1) < ROT_DIM // 2

    def rope(t, n_heads, out_ref):
        for hh in range(n_heads):
            th = t[:, LANES * hh:LANES * (hh + 1)]
            partner = jnp.where(first_half,
                                pltpu.roll(th, LANES - ROT_DIM // 2, 1),
                                pltpu.roll(th, ROT_DIM // 2, 1))
            out_ref[:, LANES * hh:LANES * (hh + 1)] = th * cosf + partner * sinf

    rope(q, ATTN_HEADS, q_ref)
    rope(k, KV_HEADS, k_ref)


def _proj(x, norm_w, w_cat, cosf, sinf, tm):
    t = x.shape[0]
    n_tab = cosf.shape[0] // tm
    row = lambda i: (i, 0)
    const = lambda i: (0, 0)
    widths = (SSM_WIDTH, XBC_WIDTH, ATTN_WIDTH, KV_WIDTH, KV_WIDTH, ATTN_WIDTH, DT_PAD)
    return pl.pallas_call(
        _proj_kernel,
        grid=(t // tm,),
        in_specs=[
            pl.BlockSpec((tm, D_MODEL), row),
            pl.BlockSpec((1, D_MODEL), const),
            pl.BlockSpec((D_MODEL, PROJ_WIDTH), const, pipeline_mode=pl.Buffered(1)),
            pl.BlockSpec((tm, LANES), lambda i: (i % n_tab, 0)),
            pl.BlockSpec((tm, LANES), lambda i: (i % n_tab, 0)),
        ],
        out_specs=[pl.BlockSpec((tm, w), row) for w in widths],
        out_shape=[jax.ShapeDtypeStruct((t, w), F32) for w in widths],
        compiler_params=pltpu.CompilerParams(
            dimension_semantics=("parallel",), vmem_limit_bytes=VMEM_LIMIT),
        name="in_proj",
    )(x, norm_w, w_cat, cosf, sinf)


def _ssd_prompt_kernel(xbc_ref, z_ref, dt_ref, cw_ref, cb_ref, dtb_ref, alog_ref, dsk_ref, nw_ref,
                       y_ref, hfin_ref, cfin_ref, h_scr, tail_scr, y_scr):
    c = pl.program_id(1)
    l = SSD_CHUNK

    @pl.when(c == 0)
    def _():
        h_scr[...] = jnp.zeros_like(h_scr)
        tail_scr[...] = jnp.zeros_like(tail_scr)

    xb = xbc_ref[...]
    xp = jnp.concatenate([tail_scr[...], xb], axis=0)
    conv = cb_ref[...]
    for i in range(CONV_WIDTH):
        lo = 8 - (CONV_WIDTH - 1) + i
        conv = conv + xp[lo:lo + l, :] * cw_ref[i:i + 1, :]
    tail_scr[...] = xb[l - 8:, :]
    u = _silu(conv)
    xs = u[:, :SSM_WIDTH]

    dt = _softplus(dt_ref[...] + dtb_ref[...])
    a_neg = -jnp.exp(alog_ref[...])
    row = lax.broadcasted_iota(jnp.int32, (l, l), 0)
    col = lax.broadcasted_iota(jnp.int32, (l, l), 1)
    causal = col <= row
    tri = jnp.where(causal, 1.0, 0.0).astype(BF16)
    hi, mid, lo3 = _split3(dt * a_neg)
    acum = _nn(tri, hi) + _nn(tri, mid) + _nn(tri, lo3)
    acum_t = acum.T
    a_last = acum[l - 1:l, :]
    dec_end = jnp.exp(a_last - acum)
    ea = jnp.exp(acum)
    chunk_decay = jnp.exp(a_last)

    for g in range(SSM_GROUPS):
        b0 = SSM_WIDTH + SSM_STATE * g
        c0 = SSM_WIDTH + SSM_GROUPS * SSM_STATE + SSM_STATE * g
        bb = u[:, b0:b0 + SSM_STATE].astype(BF16)
        cc = u[:, c0:c0 + SSM_STATE].astype(BF16)
        cb = _nt(cc, bb)
        for r in range(HEADS_PER_GROUP):
            h = g * HEADS_PER_GROUP + r
            p0 = SSM_HEADDIM * h
            seg = acum[:, h:h + 1] - acum_t[h:h + 1, :]
            lmat = jnp.exp(jnp.where(causal, seg, NEG))
            xh = xs[:, p0:p0 + SSM_HEADDIM]
            xdt = xh * dt[:, h:h + 1]
            hs = h_scr[h]
            y = _nn((cb * lmat).astype(BF16), xdt.astype(BF16))
            y = y + _nt(cc, hs.astype(BF16)) * ea[:, h:h + 1]
            y = y + dsk_ref[:, p0:p0 + SSM_HEADDIM] * xh
            st = _tn((xdt * dec_end[:, h:h + 1]).astype(BF16), bb)
            h_scr[h] = hs * chunk_decay[:, h:h + 1] + st
            y_scr[:, p0:p0 + SSM_HEADDIM] = y

    y_ref[...] = _rmsnorm(y_scr[...] * _silu(z_ref[...]), nw_ref[...])

    @pl.when(c == pl.num_programs(1) - 1)
    def _():
        hfin_ref[0] = h_scr[...]
        cfin_ref[0] = xb[l - (CONV_WIDTH - 1):, :]


def _ssd_prompt(xbc, z, dt, conv_w, conv_b, dtb, alog, dsk, nw, batch, seq):
    nc = seq // SSD_CHUNK
    row = lambda b, c: (b * nc + c, 0)
    const = lambda b, c: (0, 0)
    return pl.pallas_call(
        _ssd_prompt_kernel,
        grid=(batch, nc),
        in_specs=[
            pl.BlockSpec((SSD_CHUNK, XBC_WIDTH), row),
            pl.BlockSpec((SSD_CHUNK, SSM_WIDTH), row),
            pl.BlockSpec((SSD_CHUNK, DT_PAD), row),
            pl.BlockSpec((CONV_WIDTH, XBC_WIDTH), const),
            pl.BlockSpec((1, XBC_WIDTH), const),
            pl.BlockSpec((1, DT_PAD), const),
            pl.BlockSpec((1, DT_PAD), const),
            pl.BlockSpec((1, SSM_WIDTH), const),
            pl.BlockSpec((1, SSM_WIDTH), const),
        ],
        out_specs=[
            pl.BlockSpec((SSD_CHUNK, SSM_WIDTH), row),
            pl.BlockSpec((1, SSM_HEADS, SSM_HEADDIM, SSM_STATE), lambda b, c: (b, 0, 0, 0)),
            pl.BlockSpec((1, CONV_WIDTH - 1, XBC_WIDTH), lambda b, c: (b, 0, 0)),
        ],
        out_shape=[
            jax.ShapeDtypeStruct((batch * seq, SSM_WIDTH), F32),
            jax.ShapeDtypeStruct((batch, SSM_HEADS, SSM_HEADDIM, SSM_STATE), F32),
            jax.ShapeDtypeStruct((batch, CONV_WIDTH - 1, XBC_WIDTH), F32),
        ],
        scratch_shapes=[
            pltpu.VMEM((SSM_HEADS, SSM_HEADDIM, SSM_STATE), F32),
            pltpu.VMEM((8, XBC_WIDTH), F32),
            pltpu.VMEM((SSD_CHUNK, SSM_WIDTH), F32),
        ],
        compiler_params=pltpu.CompilerParams(
            dimension_semantics=("parallel", "arbitrary"), vmem_limit_bytes=VMEM_LIMIT),
        name="ssd_prompt",
    )(xbc, z, dt, conv_w, conv_b, dtb, alog, dsk, nw)


def _attn_prompt_kernel(q_ref, g_ref, k_ref, v_ref, o_ref, kb_scr, vb_scr, km_scr, *, k_sel):
    n = pl.program_id(2)
    seq = k_ref.shape[0]
    nblk = seq // MOBA_BLOCK
    rows = Q_PER_KV * Q_BLOCK

    @pl.when(n == 0)
    def _():
        for j in range(nblk):
            kf = k_ref[MOBA_BLOCK * j:MOBA_BLOCK * (j + 1), :]
            kb_scr[MOBA_BLOCK * j:MOBA_BLOCK * (j + 1), :] = kf.astype(BF16)
            km_scr[j:j + 1, :] = jnp.mean(kf, axis=0, keepdims=True)
            vb_scr[MOBA_BLOCK * j:MOBA_BLOCK * (j + 1), :] = (
                v_ref[MOBA_BLOCK * j:MOBA_BLOCK * (j + 1), :].astype(BF16))

    own = (n * Q_BLOCK) // MOBA_BLOCK
    qf = q_ref[...]
    q2 = jnp.concatenate([qf[:, LANES * r:LANES * (r + 1)] for r in range(Q_PER_KV)], axis=0)

    sels = []
    if k_sel > 0:
        gate = _nt_x3(q2, km_scr[...])
        colf = lax.broadcasted_iota(jnp.int32, gate.shape, 1).astype(F32)
        ownf = own.astype(F32)
        gate = jnp.where(colf < ownf, gate, NEG)
        for t in range(k_sel):
            m = jnp.max(gate, axis=-1, keepdims=True)
            idx = jnp.min(jnp.where(gate == m, colf, float(nblk)), axis=-1, keepdims=True)
            sels.append(jnp.where(t < own, idx, -1.0))
            gate = jnp.where(colf == idx, -jnp.inf, gate)

    scale = ATTN_HEADDIM ** -0.5
    qb = q2.astype(BF16)

    def attend(j, mask, carry):
        m, l, acc = carry
        start = pl.multiple_of(j * MOBA_BLOCK, MOBA_BLOCK)
        s = _nt(qb, kb_scr[pl.ds(start, MOBA_BLOCK), :]) * scale
        s = jnp.where(mask, s, NEG)
        m_new = jnp.maximum(m, jnp.max(s, axis=-1, keepdims=True))
        alpha = jnp.exp(m - m_new)
        p = jnp.exp(s - m_new)
        l = alpha * l + jnp.sum(p, axis=-1, keepdims=True)
        acc = alpha * acc + _nn(p.astype(BF16), vb_scr[pl.ds(start, MOBA_BLOCK), :])
        return m_new, l, acc

    qpos = n * Q_BLOCK + (lax.broadcasted_iota(jnp.int32, (rows, MOBA_BLOCK), 0) & (Q_BLOCK - 1))
    kpos = own * MOBA_BLOCK + lax.broadcasted_iota(jnp.int32, (rows, MOBA_BLOCK), 1)
    init = (jnp.full((rows, 1), NEG, F32), jnp.zeros((rows, 1), F32),
            jnp.zeros((rows, ATTN_HEADDIM), F32))
    carry = attend(own, kpos <= qpos, init)

    if k_sel > 0:
        def body(j, carry):
            jf = j.astype(F32)
            chosen = sels[0] == jf
            for t in range(1, k_sel):
                chosen = jnp.logical_or(chosen, sels[t] == jf)
            return attend(j, chosen, carry)
        carry = lax.fori_loop(0, own, body, carry)

    _, l, acc = carry
    o = acc / l
    gg = _silu(g_ref[...])
    for r in range(Q_PER_KV):
        o_ref[:, LANES * r:LANES * (r + 1)] = (
            o[Q_BLOCK * r:Q_BLOCK * (r + 1), :] * gg[:, LANES * r:LANES * (r + 1)])


def _attn_prompt(q, g, k, v, batch, seq):
    nq = seq // Q_BLOCK
    nblk = seq // MOBA_BLOCK
    k_sel = min(MOBA_TOPK, nblk - 1)
    qmap = lambda b, h, n: (b * nq + n, h)
    kvmap = lambda b, h, n: (b, h)
    width = Q_PER_KV * ATTN_HEADDIM
    return pl.pallas_call(
        functools.partial(_attn_prompt_kernel, k_sel=k_sel),
        grid=(batch, KV_HEADS, nq),
        in_specs=[
            pl.BlockSpec((Q_BLOCK, width), qmap),
            pl.BlockSpec((Q_BLOCK, width), qmap),
            pl.BlockSpec((seq, ATTN_HEADDIM), kvmap),
            pl.BlockSpec((seq, ATTN_HEADDIM), kvmap),
        ],
        out_specs=pl.BlockSpec((Q_BLOCK, width), qmap),
        out_shape=jax.ShapeDtypeStruct((batch * seq, ATTN_WIDTH), F32),
        scratch_shapes=[
            pltpu.VMEM((seq, ATTN_HEADDIM), BF16),
            pltpu.VMEM((seq, ATTN_HEADDIM), BF16),
            pltpu.VMEM((nblk, ATTN_HEADDIM), F32),
        ],
        compiler_params=pltpu.CompilerParams(
            dimension_semantics=("parallel", "parallel", "arbitrary"),
            vmem_limit_bytes=VMEM_LIMIT),
        name="moba_prompt",
    )(q, g, k, v)


def _out_kernel(ys_ref, ya_ref, x_ref, w1_ref, w2_ref, nw_ref, o_ref):
    out = x_ref[...] + _nn(ys_ref[...].astype(BF16), w1_ref[...])
    out = out + _nn(ya_ref[...].astype(BF16), w2_ref[...])
    o_ref[...] = _rmsnorm(out, nw_ref[...])


def _out_proj(ys, ya, x, w1, w2, nw, tm):
    t = x.shape[0]
    row = lambda i: (i, 0)
    const = lambda i: (0, 0)
    return pl.pallas_call(
        _out_kernel,
        grid=(t // tm,),
        in_specs=[
            pl.BlockSpec((tm, SSM_WIDTH), row),
            pl.BlockSpec((tm, ATTN_WIDTH), row),
            pl.BlockSpec((tm, D_MODEL), row),
            pl.BlockSpec((SSM_WIDTH, D_MODEL), const),
            pl.BlockSpec((ATTN_WIDTH, D_MODEL), const),
            pl.BlockSpec((1, D_MODEL), const),
        ],
        out_specs=pl.BlockSpec((tm, D_MODEL), row),
        out_shape=jax.ShapeDtypeStruct((t, D_MODEL), F32),
        compiler_params=pltpu.CompilerParams(
            dimension_semantics=("parallel",), vmem_limit_bytes=VMEM_LIMIT),
        name="out_proj",
    )(ys, ya, x, w1, w2, nw)


def _ssd_sample_kernel(xbc_ref, cst_ref, z_ref, dt_ref, h0_ref, cw_ref, cb_ref, dtb_ref, alog_ref,
                       dsk_ref, nw_ref, eh_ref, eg_ref,
                       y_ref, hnew_ref, cnew_ref, *, steps):
    tb = SAMPLE_TILE
    hist = CONV_WIDTH - 1
    rows = [cst_ref[:, j, :] for j in range(hist)] + [xbc_ref[:, t, :] for t in range(steps)]
    for j in range(hist):
        cnew_ref[:, j, :] = rows[steps + j]

    a_neg = -jnp.exp(alog_ref[...])
    u, dts, acs = [], [], []
    run = None
    for t in range(steps):
        conv = cb_ref[...]
        for i in range(CONV_WIDTH):
            conv = conv + rows[t + i] * cw_ref[i:i + 1, :]
        u.append(_silu(conv))
        dt = _softplus(dt_ref[:, t, :] + dtb_ref[...])
        dts.append(dt)
        run = dt * a_neg if run is None else run + dt * a_neg
        acs.append(run)
    uu = jnp.concatenate(u, axis=0)
    xs = uu[:, :SSM_WIDTH]
    bm = uu[:, SSM_WIDTH:SSM_WIDTH + SSM_GROUPS * SSM_STATE]
    cm = uu[:, SSM_WIDTH + SSM_GROUPS * SSM_STATE:]
    dt_all = jnp.concatenate(dts, axis=0)
    ac_all = jnp.concatenate(acs, axis=0)
    a_last = jnp.concatenate([acs[-1]] * steps, axis=0)

    eh = eh_ref[...]
    xdt = xs * _nn_exact_rhs(dt_all, eh)
    xw = xs * _nn_exact_rhs(dt_all * jnp.exp(a_last - ac_all), eh)
    ea = _nn_exact_rhs(jnp.exp(ac_all), eh)

    eg = eg_ref[...]
    ydiag = []
    for t in range(steps):
        acc = None
        for j in range(t + 1):
            cbx = _nn_exact_rhs(cm[tb * t:tb * (t + 1)] * bm[tb * j:tb * (j + 1)], eg)
            lmx = _nn_exact_rhs(jnp.exp(acs[t] - acs[j]), eh)
            term = cbx * lmx * xdt[tb * j:tb * (j + 1)]
            acc = term if acc is None else acc + term
        ydiag.append(acc)
    y = jnp.concatenate(ydiag, axis=0) + dsk_ref[...] * xs

    rowb = lax.broadcasted_iota(jnp.int32, (steps * tb, 1), 0) % tb
    cmb = cm.astype(BF16)
    gw = HEADS_PER_GROUP * SSM_HEADDIM
    yoff = [None] * SSM_GROUPS
    for b in range(tb):
        mine = rowb == b
        for g in range(SSM_GROUPS):
            h0g = h0_ref[b, HEADS_PER_GROUP * g:HEADS_PER_GROUP * (g + 1)].reshape(gw, SSM_STATE)
            part = jnp.where(mine, _nt(cmb[:, SSM_STATE * g:SSM_STATE * (g + 1)], h0g.astype(BF16)), 0.0)
            yoff[g] = part if yoff[g] is None else yoff[g] + part
            xwb = jnp.where(mine, xw[:, gw * g:gw * (g + 1)], 0.0)
            xh, xm, _ = _split3(xwb)
            bh, bmid, _ = _split3(bm[:, SSM_STATE * g:SSM_STATE * (g + 1)])
            st = _tn(xh, bh) + _tn(xh, bmid) + _tn(xm, bh)
            for r in range(HEADS_PER_GROUP):
                h = HEADS_PER_GROUP * g + r
                dec = jnp.exp(acs[-1][b:b + 1, h:h + 1])
                hnew_ref[b, h] = (h0g[SSM_HEADDIM * r:SSM_HEADDIM * (r + 1)] * dec
                                  + st[SSM_HEADDIM * r:SSM_HEADDIM * (r + 1)])
    y = y + jnp.concatenate(yoff, axis=1) * ea

    for t in range(steps):
        yt = y[tb * t:tb * (t + 1)] * _silu(z_ref[:, t, :])
        y_ref[:, t, :] = _rmsnorm(yt, nw_ref[...])


def _ssd_sample(xbc, cst, z, dt, h0, conv_w, conv_b, dtb, alog, dsk, nw, eh, eg):
    nb, steps = xbc.shape[0], xbc.shape[1]
    tb = SAMPLE_TILE
    c2 = lambda i: (0, 0)
    b3 = lambda i: (i, 0, 0)
    b4 = lambda i: (i, 0, 0, 0)
    hist = CONV_WIDTH - 1
    return pl.pallas_call(
        functools.partial(_ssd_sample_kernel, steps=steps),
        grid=(nb // tb,),
        in_specs=[
            pl.BlockSpec((tb, steps, XBC_WIDTH), b3),
            pl.BlockSpec((tb, hist, XBC_WIDTH), b3),
            pl.BlockSpec((tb, steps, SSM_WIDTH), b3),
            pl.BlockSpec((tb, steps, DT_PAD), b3),
            pl.BlockSpec((tb, SSM_HEADS, SSM_HEADDIM, SSM_STATE), b4),
            pl.BlockSpec((CONV_WIDTH, XBC_WIDTH), c2),
            pl.BlockSpec((1, XBC_WIDTH), c2),
            pl.BlockSpec((1, DT_PAD), c2),
            pl.BlockSpec((1, DT_PAD), c2),
            pl.BlockSpec((1, SSM_WIDTH), c2),
            pl.BlockSpec((1, SSM_WIDTH), c2),
            pl.BlockSpec((DT_PAD, SSM_WIDTH), c2),
            pl.BlockSpec((SSM_GROUPS * SSM_STATE, SSM_WIDTH), c2),
        ],
        out_specs=[
            pl.BlockSpec((tb, steps, SSM_WIDTH), b3),
            pl.BlockSpec((tb, SSM_HEADS, SSM_HEADDIM, SSM_STATE), b4),
            pl.BlockSpec((tb, hist, XBC_WIDTH), b3),
        ],
        out_shape=[
            jax.ShapeDtypeStruct((nb, steps, SSM_WIDTH), F32),
            jax.ShapeDtypeStruct((nb, SSM_HEADS, SSM_HEADDIM, SSM_STATE), F32),
            jax.ShapeDtypeStruct((nb, hist, XBC_WIDTH), F32),
        ],
        compiler_params=pltpu.CompilerParams(
            dimension_semantics=("parallel",), vmem_limit_bytes=VMEM_LIMIT),
        name="ssd_sample",
    )(xbc, cst, z, dt, h0, conv_w, conv_b, dtb, alog, dsk, nw, eh, eg)


def _attn_sample_kernel(pt_ref, q_ref, g_ref, kn_ref, vn_ref, ck_ref, cv_ref, o_ref,
                        kbuf, vbuf, km_scr, idx_v, idx_s, o_scr, sems, *, n_pages, steps):
    b = pl.program_id(0)
    n_full = n_pages // PAGES_PER_BLOCK
    page_rows = PAGE_SIZE * KV_HEADS
    n_cols = ATTN_HEADS * steps

    def k_copy(p):
        return pltpu.make_async_copy(ck_ref.at[pt_ref[b * n_pages + p]], kbuf.at[p], sems.at[0])

    def v_copy(p):
        return pltpu.make_async_copy(cv_ref.at[pt_ref[b * n_pages + p]], vbuf.at[p], sems.at[1])

    for p in range(n_pages):
        k_copy(p).start()
    for p in range(n_pages):
        v_copy(p).start()
    for p in range(n_pages):
        k_copy(p).wait()

    km_scr[...] = jnp.zeros_like(km_scr)
    for blk in range(n_full):
        tot = None
        for pp in range(PAGES_PER_BLOCK):
            part = kbuf[PAGES_PER_BLOCK * blk + pp].reshape(page_rows // 8, 8, LANES).sum(axis=0)
            tot = part if tot is None else tot + part
        km_scr[KV_HEADS * blk:KV_HEADS * (blk + 1), :] = (
            (tot[:KV_HEADS] + tot[KV_HEADS:]) * (1.0 / MOBA_BLOCK))

    qf = q_ref[0]
    q_all = jnp.concatenate([qf[:, LANES * h:LANES * (h + 1)] for h in range(ATTN_HEADS)], axis=0)
    q_pad = jnp.concatenate([q_all, jnp.zeros((LANES - n_cols, LANES), F32)], axis=0)

    gate = _nt_x3(q_pad, km_scr[...])
    lane = lax.broadcasted_iota(jnp.int32, gate.shape, 1)
    rowc = lax.broadcasted_iota(jnp.int32, gate.shape, 0)
    ok = jnp.logical_and(lane % KV_HEADS == rowc // (Q_PER_KV * steps), lane < KV_HEADS * n_full)
    gate = jnp.where(ok, gate, -jnp.inf)
    lanef = lane.astype(F32)
    chosen = jnp.zeros(gate.shape, jnp.int32)
    for t in range(MOBA_TOPK):
        m = jnp.max(gate, axis=-1, keepdims=True)
        idx = jnp.min(jnp.where(gate == m, lanef, float(4 * LANES)), axis=-1, keepdims=True)
        chosen = jnp.where(lane == t, idx.astype(jnp.int32) // KV_HEADS, chosen)
        gate = jnp.where(lanef == idx, -jnp.inf, gate)
    idx_v[...] = chosen
    to_smem = pltpu.make_async_copy(idx_v, idx_s, sems.at[2])
    to_smem.start()
    to_smem.wait()

    for p in range(n_pages):
        v_copy(p).wait()

    q_t = q_pad.T.astype(BF16)
    scale = ATTN_HEADDIM ** -0.5
    sub8 = lax.broadcasted_iota(jnp.int32, (8, LANES), 0)
    colsel = lax.broadcasted_iota(jnp.int32, (LANES, LANES), 0)
    per_kv = Q_PER_KV * steps

    for g in range(KV_HEADS):
        pad = jnp.zeros((8 - steps, LANES), F32)
        kn = jnp.concatenate([kn_ref[0][:, LANES * g:LANES * (g + 1)], pad], axis=0)
        vn = jnp.concatenate([vn_ref[0][:, LANES * g:LANES * (g + 1)], pad], axis=0)
        knb = kn.astype(BF16)

        def col_body(i, _, g=g, knb=knb, vn=vn):
            c = g * per_kv + i
            w = _nn(q_t, jnp.where(colsel == c, 1.0, 0.0).astype(BF16)).astype(BF16)

            def part(s, vt, carry):
                m, l, acc = carry
                m_new = jnp.maximum(m, jnp.max(s, axis=0, keepdims=True))
                alpha = jnp.exp(m - m_new)
                p = jnp.exp(s - m_new)
                l = alpha * l + jnp.sum(p, axis=0, keepdims=True)
                acc = alpha * acc + jnp.sum(p * vt, axis=0, keepdims=True)
                return m_new, l, acc

            s_own = _nn(knb, w) * scale
            s_own = jnp.where(sub8 <= i % steps, s_own, NEG)
            carry = (jnp.full((1, LANES), NEG, F32), jnp.zeros((1, LANES), F32),
                     jnp.zeros((1, LANES), F32))
            carry = part(s_own, vn, carry)
            for t in range(MOBA_TOPK):
                blk = idx_s[c, t]
                kt = jnp.concatenate(
                    [kbuf[PAGES_PER_BLOCK * blk + pp, pl.ds(g, PAGE_SIZE, stride=KV_HEADS), :]
                     for pp in range(PAGES_PER_BLOCK)], axis=0)
                vt = jnp.concatenate(
                    [vbuf[PAGES_PER_BLOCK * blk + pp, pl.ds(g, PAGE_SIZE, stride=KV_HEADS), :]
                     for pp in range(PAGES_PER_BLOCK)], axis=0)
                carry = part(_nn(kt.astype(BF16), w) * scale, vt, carry)
            _, l, acc = carry
            o_scr[pl.ds(c, 1), :] = acc / l
            return 0

        lax.fori_loop(0, per_kv, col_body, 0)

    gg = _silu(g_ref[0])
    for h in range(ATTN_HEADS):
        o_ref[0, :, LANES * h:LANES * (h + 1)] = (
            o_scr[steps * h:steps * (h + 1), :] * gg[:, LANES * h:LANES * (h + 1)])


def _attn_sample(page_table, q, g, k_new, v_new, cache_k, cache_v):
    nb, steps = q.shape[0], q.shape[1]
    n_pages = page_table.shape[1]
    n_pool = cache_k.shape[0]
    page_rows = PAGE_SIZE * KV_HEADS
    ck = cache_k.reshape(n_pool, page_rows, ATTN_HEADDIM)
    cv = cache_v.reshape(n_pool, page_rows, ATTN_HEADDIM)
    b3 = lambda i, pt: (i, 0, 0)
    grid_spec = pltpu.PrefetchScalarGridSpec(
        num_scalar_prefetch=1,
        grid=(nb,),
        in_specs=[
            pl.BlockSpec((1, steps, ATTN_WIDTH), b3),
            pl.BlockSpec((1, steps, ATTN_WIDTH), b3),
            pl.BlockSpec((1, steps, KV_WIDTH), b3),
            pl.BlockSpec((1, steps, KV_WIDTH), b3),
            pl.BlockSpec(memory_space=pl.ANY),
            pl.BlockSpec(memory_space=pl.ANY),
        ],
        out_specs=pl.BlockSpec((1, steps, ATTN_WIDTH), b3),
        scratch_shapes=[
            pltpu.VMEM((n_pages, page_rows, ATTN_HEADDIM), F32),
            pltpu.VMEM((n_pages, page_rows, ATTN_HEADDIM), F32),
            pltpu.VMEM((LANES, ATTN_HEADDIM), F32),
            pltpu.VMEM((LANES, LANES), jnp.int32),
            pltpu.SMEM((LANES, LANES), jnp.int32),
            pltpu.VMEM((ATTN_HEADS * steps, ATTN_HEADDIM), F32),
            pltpu.SemaphoreType.DMA((3,)),
        ],
    )
    return pl.pallas_call(
        functools.partial(_attn_sample_kernel, n_pages=n_pages, steps=steps),
        grid_spec=grid_spec,
        out_shape=jax.ShapeDtypeStruct((nb, steps, ATTN_WIDTH), F32),
        compiler_params=pltpu.CompilerParams(
            dimension_semantics=("arbitrary",), vmem_limit_bytes=VMEM_LIMIT),
        name="moba_sample",
    )(page_table.reshape(-1), q, g, k_new, v_new, ck, cv)


def _rope_tables(pos):
    inv = 1.0 / (ROPE_THETA ** (jnp.arange(0, ROT_DIM, 2, dtype=F32) / ROT_DIM))
    ang = pos.astype(F32)[:, None] * inv[None, :]
    cos, sin = jnp.cos(ang), jnp.sin(ang)
    n = pos.shape[0]
    rest = LANES - ROT_DIM
    cosf = jnp.concatenate([cos, cos, jnp.ones((n, rest), F32)], axis=1)
    sinf = jnp.concatenate([-sin, sin, jnp.zeros((n, rest), F32)], axis=1)
    return cosf, sinf


def _row_tile(t, cap):
    tm = cap
    while t % tm:
        tm //= 2
    return tm


def kernel(x_prompt, x_sample, cache_k, cache_v, state_ssm, state_conv, page_table,
           norm_w, w_in, conv_w, conv_b, dt_bias, a_log, d_skip, ssm_norm_w, w_out, final_norm_w):
    depth = w_in.shape[0]
    assert depth == 1
    bp, seq, _ = x_prompt.shape
    nb, steps, _ = x_sample.shape
    n_pages = page_table.shape[1]
    past = n_pages * PAGE_SIZE
    assert seq % MOBA_BLOCK == 0 and past % MOBA_BLOCK == 0 and steps <= 8
    assert n_pages // PAGES_PER_BLOCK >= MOBA_TOPK and nb % SAMPLE_TILE == 0
    assert KV_HEADS * (n_pages // PAGES_PER_BLOCK) <= LANES

    w = w_in[0]
    c_dt = SSM_WIDTH + XBC_WIDTH
    w_cat = jnp.concatenate(
        [w[:, :c_dt], w[:, c_dt + SSM_HEADS:], w[:, c_dt:c_dt + SSM_HEADS],
         jnp.zeros((D_MODEL, DT_PAD - SSM_HEADS), F32)], axis=1).astype(BF16)
    nw = norm_w[0].reshape(1, D_MODEL)
    pad_h = lambda v: jnp.concatenate([v, jnp.zeros((DT_PAD - SSM_HEADS,), F32)]).reshape(1, DT_PAD)
    dtb, alog = pad_h(dt_bias[0]), pad_h(a_log[0])
    dsk = jnp.repeat(d_skip[0], SSM_HEADDIM).reshape(1, SSM_WIDTH)
    snw = ssm_norm_w[0].reshape(1, SSM_WIDTH)
    cb = conv_b[0].reshape(1, XBC_WIDTH)
    w1 = w_out[0][:SSM_WIDTH].astype(BF16)
    w2 = w_out[0][SSM_WIDTH:].astype(BF16)
    fnw = final_norm_w.reshape(1, D_MODEL)

    xp = x_prompt.reshape(bp * seq, D_MODEL)
    tm_p = _row_tile(seq, 256)
    cos_p, sin_p = _rope_tables(jnp.arange(seq))
    z, xbc, q, k, v, g, dt = _proj(xp, nw, w_cat, cos_p, sin_p, tm_p)
    y_ssm, h_p, c_p = _ssd_prompt(xbc, z, dt, conv_w[0], cb, dtb, alog, dsk, snw, bp, seq)
    y_attn = _attn_prompt(q, g, k, v, bp, seq)
    y_p = _out_proj(y_ssm, y_attn, xp, w1, w2, fnw, _row_tile(bp * seq, 512))

    ts = nb * steps
    xs = x_sample.reshape(ts, D_MODEL)
    tm_s = _row_tile(ts, 256)
    cos_s, sin_s = _rope_tables(past + jnp.arange(steps))
    cos_s, sin_s = jnp.tile(cos_s, (tm_s // steps, 1)), jnp.tile(sin_s, (tm_s // steps, 1))
    zs, xbcs, qs, ks, vs, gs, dts = _proj(xs, nw, w_cat, cos_s, sin_s, tm_s)
    heads = jnp.arange(DT_PAD)[:, None]
    lanes = jnp.arange(SSM_WIDTH)[None, :]
    eh = (lanes // SSM_HEADDIM == heads).astype(BF16)
    rows_g = jnp.arange(SSM_GROUPS * SSM_STATE)[:, None] // SSM_STATE
    eg = (lanes // (HEADS_PER_GROUP * SSM_HEADDIM) == rows_g).astype(BF16)
    r3 = lambda a: a.reshape(nb, steps, a.shape[-1])
    ys_ssm, h_s, c_s = _ssd_sample(r3(xbcs), state_conv[0], r3(zs), r3(dts), state_ssm[0],
                                   conv_w[0], cb, dtb, alog, dsk, snw, eh, eg)
    ys_attn = _attn_sample(page_table, r3(qs), r3(gs), r3(ks), r3(vs), cache_k[0], cache_v[0])
    y_s = _out_proj(ys_ssm.reshape(ts, SSM_WIDTH), ys_attn.reshape(ts, ATTN_WIDTH), xs, w1, w2, fnw,
                    _row_tile(ts, 512))

    kv5 = lambda a, n, s: a.reshape(1, n, s, KV_HEADS, ATTN_HEADDIM)
    return (y_p.reshape(bp, seq, D_MODEL), y_s.reshape(nb, steps, D_MODEL),
            kv5(k, bp, seq), kv5(v, bp, seq), kv5(ks, nb, steps), kv5(vs, nb, steps),
            h_p[None], c_p[None], h_s[None], c_s[None])
```

```python
import functools

import jax
import jax.numpy as jnp
from jax import lax
from jax.experimental import pallas as pl
from jax.experimental.pallas import tpu as pltpu

F32 = jnp.float32
BF16 = jnp.bfloat16

D_MODEL = 1024
SSM_WIDTH = 1024
SSM_HEADDIM = 64
SSM_HEADS = SSM_WIDTH // SSM_HEADDIM
SSM_GROUPS = 4
HEADS_PER_GROUP = SSM_HEADS // SSM_GROUPS
SSM_STATE = 128
CONV_WIDTH = 4
SSD_CHUNK = 128
XBC_WIDTH = SSM_WIDTH + 2 * SSM_GROUPS * SSM_STATE
ATTN_HEADDIM = 128
ATTN_HEADS = 8
KV_HEADS = 4
Q_PER_KV = ATTN_HEADS // KV_HEADS
ATTN_WIDTH = ATTN_HEADS * ATTN_HEADDIM
KV_WIDTH = KV_HEADS * ATTN_HEADDIM
MOBA_BLOCK = 256
MOBA_TOPK = 3
Q_BLOCK = 128
ROT_DIM = 32
ROPE_THETA = 500000.0
PAGE_SIZE = 128
PAGES_PER_BLOCK = MOBA_BLOCK // PAGE_SIZE
EPS = 1e-6
NEG = -1e30
SOFTMAX_C = ATTN_HEADDIM ** -0.5 * 1.4426950408889634
LANES = 128
DT_PAD = LANES
PROJ_WIDTH = SSM_WIDTH + XBC_WIDTH + ATTN_WIDTH + 2 * KV_WIDTH + ATTN_WIDTH + DT_PAD
VMEM_LIMIT = 56 * 1024 * 1024
SAMPLE_TILE = 8
ATTN_GROUP = 4


def _nt(a, b):
    return lax.dot_general(a, b, (((1,), (1,)), ((), ())), preferred_element_type=F32)


def _tn(a, b):
    return lax.dot_general(a, b, (((0,), (0,)), ((), ())), preferred_element_type=F32)


def _nn(a, b):
    return jnp.dot(a, b, preferred_element_type=F32)


def _split3(x):
    hi = x.astype(BF16)
    r1 = x - hi.astype(F32)
    mid = r1.astype(BF16)
    lo = (r1 - mid.astype(F32)).astype(BF16)
    return hi, mid, lo


def _nn_exact_rhs(x, e):
    hi, mid, lo = _split3(x)
    return _nn(hi, e) + _nn(mid, e) + _nn(lo, e)


def _nt_x3(a, b):
    ah, am, _ = _split3(a)
    bh, bm, _ = _split3(b)
    return _nt(ah, bh) + _nt(ah, bm) + _nt(am, bh)


def _silu(x):
    return x / (1.0 + jnp.exp(-x))


def _softplus(x):
    return jnp.maximum(x, 0.0) + jnp.log1p(jnp.exp(-jnp.abs(x)))


def _rmsnorm(x, w):
    ms = jnp.mean(x * x, axis=-1, keepdims=True)
    return x * lax.rsqrt(ms + EPS) * w


def _proj_kernel(x_ref, nw_ref, w_ref, cos_ref, sin_ref,
                 z_ref, xbc_ref, q_ref, k_ref, v_ref, g_ref, dt_ref):
    h = _rmsnorm(x_ref[...], nw_ref[...]).astype(BF16)

    def mm(a, b):
        return _nn(h, w_ref[:, a:b])

    o = 0
    z_ref[...] = mm(o, o + SSM_WIDTH); o += SSM_WIDTH
    xbc_ref[...] = mm(o, o + XBC_WIDTH); o += XBC_WIDTH
    q = mm(o, o + ATTN_WIDTH); o += ATTN_WIDTH
    k = mm(o, o + KV_WIDTH); o += KV_WIDTH
    v_ref[...] = mm(o, o + KV_WIDTH); o += KV_WIDTH
    g_ref[...] = mm(o, o + ATTN_WIDTH); o += ATTN_WIDTH
    dt_ref[...] = mm(o, o + DT_PAD)

    cosf = cos_ref[...]
    sinf = sin_ref[...]
    first_half = lax.broadcasted_iota(jnp.int32, cosf.shape, 1) < ROT_DIM // 2

    def rope(t, n_heads, out_ref):
        for hh in range(n_heads):
            th = t[:, LANES * hh:LANES * (hh + 1)]
            partner = jnp.where(first_half,
                                pltpu.roll(th, LANES - ROT_DIM // 2, 1),
                                pltpu.roll(th, ROT_DIM // 2, 1))
            out_ref[:, LANES * hh:LANES * (hh + 1)] = th * cosf + partner * sinf

    rope(q, ATTN_HEADS, q_ref)
    rope(k, KV_HEADS, k_ref)


def _proj(x, norm_w, w_cat, cosf, sinf, tm):
    t = x.shape[0]
    n_tab = cosf.shape[0] // tm
    row = lambda i: (i, 0)
    const = lambda i: (0, 0)
    widths = (SSM_WIDTH, XBC_WIDTH, ATTN_WIDTH, KV_WIDTH, KV_WIDTH, ATTN_WIDTH, DT_PAD)
    return pl.pallas_call(
        _proj_kernel,
        grid=(t // tm,),
        in_specs=[
            pl.BlockSpec((tm, D_MODEL), row),
            pl.BlockSpec((1, D_MODEL), const),
            pl.BlockSpec((D_MODEL, PROJ_WIDTH), const, pipeline_mode=pl.Buffered(1)),
            pl.BlockSpec((tm, LANES), lambda i: (i % n_tab, 0)),
            pl.BlockSpec((tm, LANES), lambda i: (i % n_tab, 0)),
        ],
        out_specs=[pl.BlockSpec((tm, w), row) for w in widths],
        out_shape=[jax.ShapeDtypeStruct((t, w), F32) for w in widths],
        compiler_params=pltpu.CompilerParams(
            dimension_semantics=("parallel",), vmem_limit_bytes=VMEM_LIMIT),
        name="in_proj",
    )(x, norm_w, w_cat, cosf, sinf)


def _ssd_prompt_kernel(xbc_ref, z_ref, dt_ref, cw_ref, cb_ref, dtb_ref, alog_ref, dsk_ref, nw_ref,
                       y_ref, hfin_ref, cfin_ref, h_scr, tail_scr, y_scr):
    c = pl.program_id(1)
    l = SSD_CHUNK

    @pl.when(c == 0)
    def _():
        h_scr[...] = jnp.zeros_like(h_scr)
        tail_scr[...] = jnp.zeros_like(tail_scr)

    xb = xbc_ref[...]
    xp = jnp.concatenate([tail_scr[...], xb], axis=0)
    conv = cb_ref[...]
    for i in range(CONV_WIDTH):
        lo = 8 - (CONV_WIDTH - 1) + i
        conv = conv + xp[lo:lo + l, :] * cw_ref[i:i + 1, :]
    tail_scr[...] = xb[l - 8:, :]
    u = _silu(conv)
    xs = u[:, :SSM_WIDTH]

    dt = _softplus(dt_ref[...] + dtb_ref[...])
    a_neg = -jnp.exp(alog_ref[...])
    row = lax.broadcasted_iota(jnp.int32, (l, l), 0)
    col = lax.broadcasted_iota(jnp.int32, (l, l), 1)
    causal = col <= row
    tri = jnp.where(causal, 1.0, 0.0).astype(BF16)
    hi, mid, lo3 = _split3(dt * a_neg)
    acum = _nn(tri, hi) + _nn(tri, mid) + _nn(tri, lo3)
    acum_t = acum.T
    a_last = acum[l - 1:l, :]
    dec_end = jnp.exp(a_last - acum)
    ea = jnp.exp(acum)
    chunk_decay = jnp.exp(a_last)

    for g in range(SSM_GROUPS):
        b0 = SSM_WIDTH + SSM_STATE * g
        c0 = SSM_WIDTH + SSM_GROUPS * SSM_STATE + SSM_STATE * g
        bb = u[:, b0:b0 + SSM_STATE].astype(BF16)
        cc = u[:, c0:c0 + SSM_STATE].astype(BF16)
        cb = _nt(cc, bb)
        for r in range(HEADS_PER_GROUP):
            h = g * HEADS_PER_GROUP + r
            p0 = SSM_HEADDIM * h
            seg = acum[:, h:h + 1] - acum_t[h:h + 1, :]
            lmat = jnp.exp(jnp.where(causal, seg, NEG))
            xh = xs[:, p0:p0 + SSM_HEADDIM]
            xdt = xh * dt[:, h:h + 1]
            hs = h_scr[h]
            y = _nn((cb * lmat).astype(BF16), xdt.astype(BF16))
            y = y + _nt(cc, hs.astype(BF16)) * ea[:, h:h + 1]
            y = y + dsk_ref[:, p0:p0 + SSM_HEADDIM] * xh
            st = _tn((xdt * dec_end[:, h:h + 1]).astype(BF16), bb)
            h_scr[h] = hs * chunk_decay[:, h:h + 1] + st
            y_scr[:, p0:p0 + SSM_HEADDIM] = y

    y_ref[...] = _rmsnorm(y_scr[...] * _silu(z_ref[...]), nw_ref[...])

    @pl.when(c == pl.num_programs(1) - 1)
    def _():
        hfin_ref[0] = h_scr[...]
        cfin_ref[0] = xb[l - (CONV_WIDTH - 1):, :]


def _ssd_prompt(xbc, z, dt, conv_w, conv_b, dtb, alog, dsk, nw, batch, seq):
    nc = seq // SSD_CHUNK
    row = lambda b, c: (b * nc + c, 0)
    const = lambda b, c: (0, 0)
    return pl.pallas_call(
        _ssd_prompt_kernel,
        grid=(batch, nc),
        in_specs=[
            pl.BlockSpec((SSD_CHUNK, XBC_WIDTH), row),
            pl.BlockSpec((SSD_CHUNK, SSM_WIDTH), row),
            pl.BlockSpec((SSD_CHUNK, DT_PAD), row),
            pl.BlockSpec((CONV_WIDTH, XBC_WIDTH), const),
            pl.BlockSpec((1, XBC_WIDTH), const),
            pl.BlockSpec((1, DT_PAD), const),
            pl.BlockSpec((1, DT_PAD), const),
            pl.BlockSpec((1, SSM_WIDTH), const),
            pl.BlockSpec((1, SSM_WIDTH), const),
        ],
        out_specs=[
            pl.BlockSpec((SSD_CHUNK, SSM_WIDTH), row),
            pl.BlockSpec((1, SSM_HEADS, SSM_HEADDIM, SSM_STATE), lambda b, c: (b, 0, 0, 0)),
            pl.BlockSpec((1, CONV_WIDTH - 1, XBC_WIDTH), lambda b, c: (b, 0, 0)),
        ],
        out_shape=[
            jax.ShapeDtypeStruct((batch * seq, SSM_WIDTH), F32),
            jax.ShapeDtypeStruct((batch, SSM_HEADS, SSM_HEADDIM, SSM_STATE), F32),
            jax.ShapeDtypeStruct((batch, CONV_WIDTH - 1, XBC_WIDTH), F32),
        ],
        scratch_shapes=[
            pltpu.VMEM((SSM_HEADS, SSM_HEADDIM, SSM_STATE), F32),
            pltpu.VMEM((8, XBC_WIDTH), F32),
            pltpu.VMEM((SSD_CHUNK, SSM_WIDTH), F32),
        ],
        compiler_params=pltpu.CompilerParams(
            dimension_semantics=("parallel", "arbitrary"), vmem_limit_bytes=VMEM_LIMIT),
        name="ssd_prompt",
    )(xbc, z, dt, conv_w, conv_b, dtb, alog, dsk, nw)


def _attn_prompt_kernel(q_ref, g_ref, k_ref, v_ref, o_ref, kb_scr, vt_scr, km_scr, st_scr, *, k_sel):
    n = pl.program_id(2)
    nblk = kb_scr.shape[0]

    @pl.when(n == 0)
    def _():
        for j in range(nblk):
            kf = k_ref[MOBA_BLOCK * j:MOBA_BLOCK * (j + 1), :]
            kb_scr[j] = kf.astype(BF16)
            km_scr[j:j + 1, :] = jnp.mean(kf, axis=0, keepdims=True)
            vt_scr[j] = v_ref[MOBA_BLOCK * j:MOBA_BLOCK * (j + 1), :].T.astype(BF16)

    own = (n * Q_BLOCK) // MOBA_BLOCK
    qf = q_ref[...]
    qt = jnp.concatenate([qf[:, LANES * r:LANES * (r + 1)].T for r in range(Q_PER_KV)], axis=1)
    qtb = qt.astype(BF16)
    n_cols = Q_PER_KV * Q_BLOCK

    sels = []
    if k_sel > 0:
        kh, kmid, _ = _split3(km_scr[...])
        qh, qmid, _ = _split3(qt)
        gate = _nn(kh, qh) + _nn(kh, qmid) + _nn(kmid, qh)
        rowf = lax.broadcasted_iota(jnp.int32, gate.shape, 0).astype(F32)
        gate = jnp.where(rowf < own.astype(F32), gate, NEG)
        for t in range(k_sel):
            m = jnp.max(gate, axis=0, keepdims=True)
            idx = jnp.min(jnp.where(gate == m, rowf, float(nblk)), axis=0, keepdims=True)
            sels.append(jnp.where(t < own, idx, -1.0))
            gate = jnp.where(rowf == idx, -jnp.inf, gate)

    def scores(j):
        return _nn(kb_scr[j], qtb) * SOFTMAX_C

    def chose(j):
        jf = j.astype(F32)
        hit = sels[0] == jf
        for t in range(1, k_sel):
            hit = jnp.logical_or(hit, sels[t] == jf)
        return hit

    own_slot = nblk
    kpos = own * MOBA_BLOCK + lax.broadcasted_iota(jnp.int32, (MOBA_BLOCK, n_cols), 0)
    qpos = n * Q_BLOCK + (lax.broadcasted_iota(jnp.int32, (MOBA_BLOCK, n_cols), 1) & (Q_BLOCK - 1))
    st = jnp.where(kpos <= qpos, scores(own), NEG)
    st_scr[own_slot] = st
    mx = jnp.max(st, axis=0, keepdims=True)
    n_groups = (own + ATTN_GROUP - 1) // ATTN_GROUP

    if k_sel > 0:
        def pass1(jj, mx):
            for u in range(ATTN_GROUP):
                j = jj * ATTN_GROUP + u
                jc = jnp.minimum(j, nblk - 1)
                st = scores(jc)
                st_scr[jc] = st
                mx = jnp.maximum(mx, jnp.where(chose(j), jnp.max(st, axis=0, keepdims=True), NEG))
            return mx
        mx = lax.fori_loop(0, n_groups, pass1, mx)

    def accumulate(slot, vt, msub, carry):
        l, acc = carry
        p = jnp.exp2(st_scr[slot] - msub)
        return l + jnp.sum(p, axis=0, keepdims=True), acc + _nn(vt, p.astype(BF16))

    carry = accumulate(own_slot, vt_scr[own], mx,
                       (jnp.zeros((1, n_cols), F32), jnp.zeros((ATTN_HEADDIM, n_cols), F32)))

    if k_sel > 0:
        def pass2(jj, carry):
            for u in range(ATTN_GROUP):
                j = jj * ATTN_GROUP + u
                jc = jnp.minimum(j, nblk - 1)
                carry = accumulate(jc, vt_scr[jc], jnp.where(chose(j), mx, -NEG), carry)
            return carry
        carry = lax.fori_loop(0, n_groups, pass2, carry)

    l, acc = carry
    o = acc / l
    gg = _silu(g_ref[...])
    for r in range(Q_PER_KV):
        o_ref[:, LANES * r:LANES * (r + 1)] = (
            o[:, Q_BLOCK * r:Q_BLOCK * (r + 1)].T * gg[:, LANES * r:LANES * (r + 1)])


def _attn_prompt(q, g, k, v, batch, seq):
    nq = seq // Q_BLOCK
    nblk = seq // MOBA_BLOCK
    k_sel = min(MOBA_TOPK, nblk - 1)
    qmap = lambda b, h, n: (b * nq + n, h)
    kvmap = lambda b, h, n: (b, h)
    width = Q_PER_KV * ATTN_HEADDIM
    return pl.pallas_call(
        functools.partial(_attn_prompt_kernel, k_sel=k_sel),
        grid=(batch, KV_HEADS, nq),
        in_specs=[
            pl.BlockSpec((Q_BLOCK, width), qmap),
            pl.BlockSpec((Q_BLOCK, width), qmap),
            pl.BlockSpec((seq, ATTN_HEADDIM), kvmap),
            pl.BlockSpec((seq, ATTN_HEADDIM), kvmap),
        ],
        out_specs=pl.BlockSpec((Q_BLOCK, width), qmap),
        out_shape=jax.ShapeDtypeStruct((batch * seq, ATTN_WIDTH), F32),
        scratch_shapes=[
            pltpu.VMEM((nblk, MOBA_BLOCK, ATTN_HEADDIM), BF16),
            pltpu.VMEM((nblk, ATTN_HEADDIM, MOBA_BLOCK), BF16),
            pltpu.VMEM((nblk, ATTN_HEADDIM), F32),
            pltpu.VMEM((nblk + 1, MOBA_BLOCK, Q_PER_KV * Q_BLOCK), F32),
        ],
        compiler_params=pltpu.CompilerParams(
            dimension_semantics=("parallel", "parallel", "arbitrary"),
            vmem_limit_bytes=VMEM_LIMIT),
        name="moba_prompt",
    )(q, g, k, v)


def _out_kernel(ys_ref, ya_ref, x_ref, w1_ref, w2_ref, nw_ref, o_ref):
    out = x_ref[...] + _nn(ys_ref[...].astype(BF16), w1_ref[...])
    out = out + _nn(ya_ref[...].astype(BF16), w2_ref[...])
    o_ref[...] = _rmsnorm(out, nw_ref[...])


def _out_proj(ys, ya, x, w1, w2, nw, tm):
    t = x.shape[0]
    row = lambda i: (i, 0)
    const = lambda i: (0, 0)
    return pl.pallas_call(
        _out_kernel,
        grid=(t // tm,),
        in_specs=[
            pl.BlockSpec((tm, SSM_WIDTH), row),
            pl.BlockSpec((tm, ATTN_WIDTH), row),
            pl.BlockSpec((tm, D_MODEL), row),
            pl.BlockSpec((SSM_WIDTH, D_MODEL), const),
            pl.BlockSpec((ATTN_WIDTH, D_MODEL), const),
            pl.BlockSpec((1, D_MODEL), const),
        ],
        out_specs=pl.BlockSpec((tm, D_MODEL), row),
        out_shape=jax.ShapeDtypeStruct((t, D_MODEL), F32),
        compiler_params=pltpu.CompilerParams(
            dimension_semantics=("parallel",), vmem_limit_bytes=VMEM_LIMIT),
        name="out_proj",
    )(ys, ya, x, w1, w2, nw)


def _ssd_sample_kernel(xbc_ref, cst_ref, z_ref, dt_ref, h0_ref, cw_ref, cb_ref, dtb_ref, alog_ref,
                       dsk_ref, nw_ref, eh_ref, eg_ref,
                       y_ref, hnew_ref, cnew_ref, *, steps):
    tb = SAMPLE_TILE
    hist = CONV_WIDTH - 1
    rows = [cst_ref[:, j, :] for j in range(hist)] + [xbc_ref[:, t, :] for t in range(steps)]
    for j in range(hist):
        cnew_ref[:, j, :] = rows[steps + j]

    a_neg = -jnp.exp(alog_ref[...])
    u, dts, acs = [], [], []
    run = None
    for t in range(steps):
        conv = cb_ref[...]
        for i in range(CONV_WIDTH):
            conv = conv + rows[t + i] * cw_ref[i:i + 1, :]
        u.append(_silu(conv))
        dt = _softplus(dt_ref[:, t, :] + dtb_ref[...])
        dts.append(dt)
        run = dt * a_neg if run is None else run + dt * a_neg
        acs.append(run)
    uu = jnp.concatenate(u, axis=0)
    xs = uu[:, :SSM_WIDTH]
    bm = uu[:, SSM_WIDTH:SSM_WIDTH + SSM_GROUPS * SSM_STATE]
    cm = uu[:, SSM_WIDTH + SSM_GROUPS * SSM_STATE:]
    dt_all = jnp.concatenate(dts, axis=0)
    ac_all = jnp.concatenate(acs, axis=0)
    a_last = jnp.concatenate([acs[-1]] * steps, axis=0)

    eh = eh_ref[...]
    xdt = xs * _nn_exact_rhs(dt_all, eh)
    xw = xs * _nn_exact_rhs(dt_all * jnp.exp(a_last - ac_all), eh)
    ea = _nn_exact_rhs(jnp.exp(ac_all), eh)

    eg = eg_ref[...]
    ydiag = []
    for t in range(steps):
        acc = None
        for j in range(t + 1):
            cbx = _nn_exact_rhs(cm[tb * t:tb * (t + 1)] * bm[tb * j:tb * (j + 1)], eg)
            lmx = _nn_exact_rhs(jnp.exp(acs[t] - acs[j]), eh)
            term = cbx * lmx * xdt[tb * j:tb * (j + 1)]
            acc = term if acc is None else acc + term
        ydiag.append(acc)
    y = jnp.concatenate(ydiag, axis=0) + dsk_ref[...] * xs

    rowb = lax.broadcasted_iota(jnp.int32, (steps * tb, 1), 0) % tb
    cmb = cm.astype(BF16)
    gw = HEADS_PER_GROUP * SSM_HEADDIM
    yoff = [None] * SSM_GROUPS
    for b in range(tb):
        mine = rowb == b
        for g in range(SSM_GROUPS):
            h0g = h0_ref[b, HEADS_PER_GROUP * g:HEADS_PER_GROUP * (g + 1)].reshape(gw, SSM_STATE)
            part = jnp.where(mine, _nt(cmb[:, SSM_STATE * g:SSM_STATE * (g + 1)], h0g.astype(BF16)), 0.0)
            yoff[g] = part if yoff[g] is None else yoff[g] + part
            xwb = jnp.where(mine, xw[:, gw * g:gw * (g + 1)], 0.0)
            xh, xm, _ = _split3(xwb)
            bh, bmid, _ = _split3(bm[:, SSM_STATE * g:SSM_STATE * (g + 1)])
            st = _tn(xh, bh) + _tn(xh, bmid) + _tn(xm, bh)
            for r in range(HEADS_PER_GROUP):
                h = HEADS_PER_GROUP * g + r
                dec = jnp.exp(acs[-1][b:b + 1, h:h + 1])
                hnew_ref[b, h] = (h0g[SSM_HEADDIM * r:SSM_HEADDIM * (r + 1)] * dec
                                  + st[SSM_HEADDIM * r:SSM_HEADDIM * (r + 1)])
    y = y + jnp.concatenate(yoff, axis=1) * ea

    for t in range(steps):
        yt = y[tb * t:tb * (t + 1)] * _silu(z_ref[:, t, :])
        y_ref[:, t, :] = _rmsnorm(yt, nw_ref[...])


def _ssd_sample(xbc, cst, z, dt, h0, conv_w, conv_b, dtb, alog, dsk, nw, eh, eg):
    nb, steps = xbc.shape[0], xbc.shape[1]
    tb = SAMPLE_TILE
    c2 = lambda i: (0, 0)
    b3 = lambda i: (i, 0, 0)
    b4 = lambda i: (i, 0, 0, 0)
    hist = CONV_WIDTH - 1
    return pl.pallas_call(
        functools.partial(_ssd_sample_kernel, steps=steps),
        grid=(nb // tb,),
        in_specs=[
            pl.BlockSpec((tb, steps, XBC_WIDTH), b3),
            pl.BlockSpec((tb, hist, XBC_WIDTH), b3),
            pl.BlockSpec((tb, steps, SSM_WIDTH), b3),
            pl.BlockSpec((tb, steps, DT_PAD), b3),
            pl.BlockSpec((tb, SSM_HEADS, SSM_HEADDIM, SSM_STATE), b4),
            pl.BlockSpec((CONV_WIDTH, XBC_WIDTH), c2),
            pl.BlockSpec((1, XBC_WIDTH), c2),
            pl.BlockSpec((1, DT_PAD), c2),
            pl.BlockSpec((1, DT_PAD), c2),
            pl.BlockSpec((1, SSM_WIDTH), c2),
            pl.BlockSpec((1, SSM_WIDTH), c2),
            pl.BlockSpec((DT_PAD, SSM_WIDTH), c2),
            pl.BlockSpec((SSM_GROUPS * SSM_STATE, SSM_WIDTH), c2),
        ],
        out_specs=[
            pl.BlockSpec((tb, steps, SSM_WIDTH), b3),
            pl.BlockSpec((tb, SSM_HEADS, SSM_HEADDIM, SSM_STATE), b4),
            pl.BlockSpec((tb, hist, XBC_WIDTH), b3),
        ],
        out_shape=[
            jax.ShapeDtypeStruct((nb, steps, SSM_WIDTH), F32),
            jax.ShapeDtypeStruct((nb, SSM_HEADS, SSM_HEADDIM, SSM_STATE), F32),
            jax.ShapeDtypeStruct((nb, hist, XBC_WIDTH), F32),
        ],
        compiler_params=pltpu.CompilerParams(
            dimension_semantics=("parallel",), vmem_limit_bytes=VMEM_LIMIT),
        name="ssd_sample",
    )(xbc, cst, z, dt, h0, conv_w, conv_b, dtb, alog, dsk, nw, eh, eg)


def _attn_sample_kernel(pt_ref, q_ref, g_ref, kn_ref, vn_ref, ck_ref, cv_ref, o_ref,
                        kbuf, vbuf, km_scr, w_scr, idx_v, idx_s, o_scr, sems, *, n_pages, steps):
    b = pl.program_id(0)
    slot = b % 2
    n_full = n_pages // PAGES_PER_BLOCK
    page_rows = PAGE_SIZE * KV_HEADS
    n_cols = ATTN_HEADS * steps
    per_kv = Q_PER_KV * steps

    def k_copy(seq, sl, p):
        return pltpu.make_async_copy(ck_ref.at[pt_ref[seq * n_pages + p]], kbuf.at[sl, p], sems.at[sl])

    def v_copy(p):
        return pltpu.make_async_copy(cv_ref.at[pt_ref[b * n_pages + p]], vbuf.at[p], sems.at[2])

    @pl.when(b == 0)
    def _():
        for p in range(n_pages):
            k_copy(0, 0, p).start()

    for p in range(n_pages):
        v_copy(p).start()

    @pl.when(b + 1 < pl.num_programs(0))
    def _():
        for p in range(n_pages):
            k_copy(b + 1, 1 - slot, p).start()

    for p in range(n_pages):
        k_copy(b, slot, p).wait()

    km_scr[...] = jnp.zeros_like(km_scr)
    for blk in range(n_full):
        tot = None
        for pp in range(PAGES_PER_BLOCK):
            page = kbuf[slot, PAGES_PER_BLOCK * blk + pp]
            part = page.reshape(page_rows // 8, 8, LANES).sum(axis=0)
            tot = part if tot is None else tot + part
        km_scr[KV_HEADS * blk:KV_HEADS * (blk + 1), :] = (
            (tot[:KV_HEADS] + tot[KV_HEADS:]) * (1.0 / MOBA_BLOCK))

    qf = q_ref[0]
    q_all = jnp.concatenate([qf[:, LANES * h:LANES * (h + 1)] for h in range(ATTN_HEADS)], axis=0)
    q_pad = jnp.concatenate([q_all, jnp.zeros((LANES - n_cols, LANES), F32)], axis=0)

    gate = _nt_x3(q_pad, km_scr[...])
    lane = lax.broadcasted_iota(jnp.int32, gate.shape, 1)
    rowc = lax.broadcasted_iota(jnp.int32, gate.shape, 0)
    ok = jnp.logical_and(lane % KV_HEADS == rowc // per_kv, lane < KV_HEADS * n_full)
    gate = jnp.where(ok, gate, -jnp.inf)
    lanef = lane.astype(F32)
    chosen = jnp.zeros(gate.shape, jnp.int32)
    for t in range(MOBA_TOPK):
        m = jnp.max(gate, axis=-1, keepdims=True)
        idx = jnp.min(jnp.where(gate == m, lanef, float(4 * LANES)), axis=-1, keepdims=True)
        chosen = jnp.where(lane == t, idx.astype(jnp.int32) // KV_HEADS, chosen)
        gate = jnp.where(lanef == idx, -jnp.inf, gate)
    idx_v[...] = chosen
    to_smem = pltpu.make_async_copy(idx_v, idx_s, sems.at[3])
    to_smem.start()

    q_t = q_pad.T
    for c in range(n_cols):
        w_scr[c] = jnp.broadcast_to(q_t[:, c:c + 1], (ATTN_HEADDIM, LANES)).astype(BF16)

    to_smem.wait()
    for p in range(n_pages):
        v_copy(p).wait()

    sub8 = lax.broadcasted_iota(jnp.int32, (8, LANES), 0)

    def k_tile(blk, g):
        return jnp.concatenate(
            [kbuf[slot, PAGES_PER_BLOCK * blk + pp, pl.ds(g, PAGE_SIZE, stride=KV_HEADS), :]
             for pp in range(PAGES_PER_BLOCK)], axis=0)

    def v_tile(blk, g):
        return jnp.concatenate(
            [vbuf[PAGES_PER_BLOCK * blk + pp, pl.ds(g, PAGE_SIZE, stride=KV_HEADS), :]
             for pp in range(PAGES_PER_BLOCK)], axis=0)

    def column(c, i, g, knb, vn):
        w = w_scr[c]
        s_own = jnp.where(sub8 <= i % steps, _nn(knb, w) * SOFTMAX_C, NEG)
        mx = jnp.max(s_own, axis=0, keepdims=True)
        blks = [idx_s[c, t] for t in range(MOBA_TOPK)]
        scores = []
        for blk in blks:
            s = _nn(k_tile(blk, g).astype(BF16), w) * SOFTMAX_C
            scores.append(s)
            mx = jnp.maximum(mx, jnp.max(s, axis=0, keepdims=True))
        p = jnp.exp2(s_own - mx)
        l = jnp.sum(p, axis=0, keepdims=True)
        acc = jnp.sum(p * vn, axis=0, keepdims=True)
        for blk, s in zip(blks, scores):
            p = jnp.exp2(s - mx)
            l = l + jnp.sum(p, axis=0, keepdims=True)
            acc = acc + jnp.sum(p * v_tile(blk, g), axis=0, keepdims=True)
        return acc / l

    for g in range(KV_HEADS):
        pad = jnp.zeros((8 - steps, LANES), F32)
        kn = jnp.concatenate([kn_ref[0][:, LANES * g:LANES * (g + 1)], pad], axis=0)
        vn = jnp.concatenate([vn_ref[0][:, LANES * g:LANES * (g + 1)], pad], axis=0)
        knb = kn.astype(BF16)

        def pair_body(ip, _, g=g, knb=knb, vn=vn):
            for u in range(2):
                i = 2 * ip + u
                c = g * per_kv + i
                o_scr[pl.ds(c, 1), :] = column(c, i, g, knb, vn)
            return 0

        lax.fori_loop(0, per_kv // 2, pair_body, 0)

    gg = _silu(g_ref[0])
    for h in range(ATTN_HEADS):
        o_ref[0, :, LANES * h:LANES * (h + 1)] = (
            o_scr[steps * h:steps * (h + 1), :] * gg[:, LANES * h:LANES * (h + 1)])


def _attn_sample(page_table, q, g, k_new, v_new, cache_k, cache_v):
    nb, steps = q.shape[0], q.shape[1]
    n_pages = page_table.shape[1]
    n_pool = cache_k.shape[0]
    page_rows = PAGE_SIZE * KV_HEADS
    ck = cache_k.reshape(n_pool, page_rows, ATTN_HEADDIM)
    cv = cache_v.reshape(n_pool, page_rows, ATTN_HEADDIM)
    b3 = lambda i, pt: (i, 0, 0)
    grid_spec = pltpu.PrefetchScalarGridSpec(
        num_scalar_prefetch=1,
        grid=(nb,),
        in_specs=[
            pl.BlockSpec((1, steps, ATTN_WIDTH), b3),
            pl.BlockSpec((1, steps, ATTN_WIDTH), b3),
            pl.BlockSpec((1, steps, KV_WIDTH), b3),
            pl.BlockSpec((1, steps, KV_WIDTH), b3),
            pl.BlockSpec(memory_space=pl.ANY),
            pl.BlockSpec(memory_space=pl.ANY),
        ],
        out_specs=pl.BlockSpec((1, steps, ATTN_WIDTH), b3),
        scratch_shapes=[
            pltpu.VMEM((2, n_pages, page_rows, ATTN_HEADDIM), F32),
            pltpu.VMEM((n_pages, page_rows, ATTN_HEADDIM), F32),
            pltpu.VMEM((LANES, ATTN_HEADDIM), F32),
            pltpu.VMEM((ATTN_HEADS * steps, ATTN_HEADDIM, LANES), BF16),
            pltpu.VMEM((LANES, LANES), jnp.int32),
            pltpu.SMEM((LANES, LANES), jnp.int32),
            pltpu.VMEM((ATTN_HEADS * steps, ATTN_HEADDIM), F32),
            pltpu.SemaphoreType.DMA((4,)),
        ],
    )
    return pl.pallas_call(
        functools.partial(_attn_sample_kernel, n_pages=n_pages, steps=steps),
        grid_spec=grid_spec,
        out_shape=jax.ShapeDtypeStruct((nb, steps, ATTN_WIDTH), F32),
        compiler_params=pltpu.CompilerParams(
            dimension_semantics=("arbitrary",), vmem_limit_bytes=VMEM_LIMIT),
        name="moba_sample",
    )(page_table.reshape(-1), q, g, k_new, v_new, ck, cv)


def _rope_tables(pos):
    inv = 1.0 / (ROPE_THETA ** (jnp.arange(0, ROT_DIM, 2, dtype=F32) / ROT_DIM))
    ang = pos.astype(F32)[:, None] * inv[None, :]
    cos, sin = jnp.cos(ang), jnp.sin(ang)
    n = pos.shape[0]
    rest = LANES - ROT_DIM
    cosf = jnp.concatenate([cos, cos, jnp.ones((n, rest), F32)], axis=1)
    sinf = jnp.concatenate([-sin, sin, jnp.zeros((n, rest), F32)], axis=1)
    return cosf, sinf


def _row_tile(t, cap):
    tm = cap
    while t % tm:
        tm //= 2
    return tm


def kernel(x_prompt, x_sample, cache_k, cache_v, state_ssm, state_conv, page_table,
           norm_w, w_in, conv_w, conv_b, dt_bias, a_log, d_skip, ssm_norm_w, w_out, final_norm_w):
    depth = w_in.shape[0]
    assert depth == 1
    bp, seq, _ = x_prompt.shape
    nb, steps, _ = x_sample.shape
    n_pages = page_table.shape[1]
    past = n_pages * PAGE_SIZE
    assert seq % MOBA_BLOCK == 0 and past % MOBA_BLOCK == 0 and steps <= 8
    assert n_pages // PAGES_PER_BLOCK >= MOBA_TOPK and nb % SAMPLE_TILE == 0
    assert KV_HEADS * (n_pages // PAGES_PER_BLOCK) <= LANES

    w = w_in[0]
    c_dt = SSM_WIDTH + XBC_WIDTH
    w_cat = jnp.concatenate(
        [w[:, :c_dt], w[:, c_dt + SSM_HEADS:], w[:, c_dt:c_dt + SSM_HEADS],
         jnp.zeros((D_MODEL, DT_PAD - SSM_HEADS), F32)], axis=1).astype(BF16)
    nw = norm_w[0].reshape(1, D_MODEL)
    pad_h = lambda v: jnp.concatenate([v, jnp.zeros((DT_PAD - SSM_HEADS,), F32)]).reshape(1, DT_PAD)
    dtb, alog = pad_h(dt_bias[0]), pad_h(a_log[0])
    dsk = jnp.repeat(d_skip[0], SSM_HEADDIM).reshape(1, SSM_WIDTH)
    snw = ssm_norm_w[0].reshape(1, SSM_WIDTH)
    cb = conv_b[0].reshape(1, XBC_WIDTH)
    w1 = w_out[0][:SSM_WIDTH].astype(BF16)
    w2 = w_out[0][SSM_WIDTH:].astype(BF16)
    fnw = final_norm_w.reshape(1, D_MODEL)

    xp = x_prompt.reshape(bp * seq, D_MODEL)
    tm_p = _row_tile(seq, 256)
    cos_p, sin_p = _rope_tables(jnp.arange(seq))
    z, xbc, q, k, v, g, dt = _proj(xp, nw, w_cat, cos_p, sin_p, tm_p)
    y_ssm, h_p, c_p = _ssd_prompt(xbc, z, dt, conv_w[0], cb, dtb, alog, dsk, snw, bp, seq)
    y_attn = _attn_prompt(q, g, k, v, bp, seq)
    y_p = _out_proj(y_ssm, y_attn, xp, w1, w2, fnw, _row_tile(bp * seq, 512))

    ts = nb * steps
    xs = x_sample.reshape(ts, D_MODEL)
    tm_s = _row_tile(ts, 256)
    cos_s, sin_s = _rope_tables(past + jnp.arange(steps))
    cos_s, sin_s = jnp.tile(cos_s, (tm_s // steps, 1)), jnp.tile(sin_s, (tm_s // steps, 1))
    zs, xbcs, qs, ks, vs, gs, dts = _proj(xs, nw, w_cat, cos_s, sin_s, tm_s)
    heads = jnp.arange(DT_PAD)[:, None]
    lanes = jnp.arange(SSM_WIDTH)[None, :]
    eh = (lanes // SSM_HEADDIM == heads).astype(BF16)
    rows_g = jnp.arange(SSM_GROUPS * SSM_STATE)[:, None] // SSM_STATE
    eg = (lanes // (HEADS_PER_GROUP * SSM_HEADDIM) == rows_g).astype(BF16)
    r3 = lambda a: a.reshape(nb, steps, a.shape[-1])
    ys_ssm, h_s, c_s = _ssd_sample(r3(xbcs), state_conv[0], r3(zs), r3(dts), state_ssm[0],
                                   conv_w[0], cb, dtb, alog, dsk, snw, eh, eg)
    ys_attn = _attn_sample(page_table, r3(qs), r3(gs), r3(ks), r3(vs), cache_k[0], cache_v[0])
    y_s = _out_proj(ys_ssm.reshape(ts, SSM_WIDTH), ys_attn.reshape(ts, ATTN_WIDTH), xs, w1, w2, fnw,
                    _row_tile(ts, 512))

    kv5 = lambda a, n, s: a.reshape(1, n, s, KV_HEADS, ATTN_HEADDIM)
    return (y_p.reshape(bp, seq, D_MODEL), y_s.reshape(nb, steps, D_MODEL),
            kv5(k, bp, seq), kv5(v, bp, seq), kv5(ks, nb, steps), kv5(vs, nb, steps),
            h_p[None], c_p[None], h_s[None], c_s[None])
```

```python
import functools

import jax
import jax.numpy as jnp
from jax import lax
from jax.experimental import pallas as pl
from jax.experimental.pallas import tpu as pltpu

F32 = jnp.float32
BF16 = jnp.bfloat16

D_MODEL = 1024
SSM_WIDTH = 1024
SSM_HEADDIM = 64
SSM_HEADS = SSM_WIDTH // SSM_HEADDIM
SSM_GROUPS = 4
HEADS_PER_GROUP = SSM_HEADS // SSM_GROUPS
SSM_STATE = 128
CONV_WIDTH = 4
SSD_CHUNK = 128
XBC_WIDTH = SSM_WIDTH + 2 * SSM_GROUPS * SSM_STATE
ATTN_HEADDIM = 128
ATTN_HEADS = 8
KV_HEADS = 4
Q_PER_KV = ATTN_HEADS // KV_HEADS
ATTN_WIDTH = ATTN_HEADS * ATTN_HEADDIM
KV_WIDTH = KV_HEADS * ATTN_HEADDIM
MOBA_BLOCK = 256
MOBA_TOPK = 3
Q_TILE = 256
ROT_DIM = 32
ROPE_THETA = 500000.0
PAGE_SIZE = 128
PAGES_PER_BLOCK = MOBA_BLOCK // PAGE_SIZE
EPS = 1e-6
NEG = -1e30
SOFTMAX_C = ATTN_HEADDIM ** -0.5 * 1.4426950408889634
LANES = 128
DT_PAD = LANES
PROJ_WIDTH = SSM_WIDTH + XBC_WIDTH + ATTN_WIDTH + 2 * KV_WIDTH + ATTN_WIDTH + DT_PAD
VMEM_LIMIT = 56 * 1024 * 1024
SAMPLE_TILE = 8
ATTN_GROUP = 4
SUM_ROWS = 16


def _nt(a, b):
    return lax.dot_general(a, b, (((1,), (1,)), ((), ())), preferred_element_type=F32)


def _tn(a, b):
    return lax.dot_general(a, b, (((0,), (0,)), ((), ())), preferred_element_type=F32)


def _nn(a, b):
    return jnp.dot(a, b, preferred_element_type=F32)


def _split3(x):
    hi = x.astype(BF16)
    r1 = x - hi.astype(F32)
    mid = r1.astype(BF16)
    lo = (r1 - mid.astype(F32)).astype(BF16)
    return hi, mid, lo


def _nn_exact_rhs(x, e):
    hi, mid, lo = _split3(x)
    return _nn(hi, e) + _nn(mid, e) + _nn(lo, e)


def _nt_x3(a, b):
    ah, am, _ = _split3(a)
    bh, bm, _ = _split3(b)
    return _nt(ah, bh) + _nt(ah, bm) + _nt(am, bh)


def _silu(x):
    return x / (1.0 + jnp.exp(-x))


def _softplus(x):
    return jnp.maximum(x, 0.0) + jnp.log1p(jnp.exp(-jnp.abs(x)))


def _rmsnorm(x, w):
    ms = jnp.mean(x * x, axis=-1, keepdims=True)
    return x * lax.rsqrt(ms + EPS) * w


def _proj_kernel(x_ref, nw_ref, w_ref, cos_ref, sin_ref,
                 z_ref, xbc_ref, q_ref, k_ref, v_ref, g_ref, dt_ref):
    h = _rmsnorm(x_ref[...], nw_ref[...]).astype(BF16)

    def mm(a, b):
        return _nn(h, w_ref[:, a:b])

    o = 0
    z_ref[...] = mm(o, o + SSM_WIDTH); o += SSM_WIDTH
    xbc_ref[...] = mm(o, o + XBC_WIDTH); o += XBC_WIDTH
    q = mm(o, o + ATTN_WIDTH); o += ATTN_WIDTH
    k = mm(o, o + KV_WIDTH); o += KV_WIDTH
    v_ref[...] = mm(o, o + KV_WIDTH); o += KV_WIDTH
    g_ref[...] = mm(o, o + ATTN_WIDTH); o += ATTN_WIDTH
    dt_ref[...] = mm(o, o + DT_PAD)

    cosf = cos_ref[...]
    sinf = sin_ref[...]
    first_half = lax.broadcasted_iota(jnp.int32, cosf.shape, 1) < ROT_DIM // 2

    def rope(t, n_heads, out_ref):
        for hh in range(n_heads):
            th = t[:, LANES * hh:LANES * (hh + 1)]
            partner = jnp.where(first_half,
                                pltpu.roll(th, LANES - ROT_DIM // 2, 1),
                                pltpu.roll(th, ROT_DIM // 2, 1))
            out_ref[:, LANES * hh:LANES * (hh + 1)] = th * cosf + partner * sinf

    rope(q, ATTN_HEADS, q_ref)
    rope(k, KV_HEADS, k_ref)


def _proj(x, norm_w, w_cat, cosf, sinf, tm):
    t = x.shape[0]
    n_tab = cosf.shape[0] // tm
    row = lambda i: (i, 0)
    const = lambda i: (0, 0)
    widths = (SSM_WIDTH, XBC_WIDTH, ATTN_WIDTH, KV_WIDTH, KV_WIDTH, ATTN_WIDTH, DT_PAD)
    return pl.pallas_call(
        _proj_kernel,
        grid=(t // tm,),
        in_specs=[
            pl.BlockSpec((tm, D_MODEL), row),
            pl.BlockSpec((1, D_MODEL), const),
            pl.BlockSpec((D_MODEL, PROJ_WIDTH), const, pipeline_mode=pl.Buffered(1)),
            pl.BlockSpec((tm, LANES), lambda i: (i % n_tab, 0)),
            pl.BlockSpec((tm, LANES), lambda i: (i % n_tab, 0)),
        ],
        out_specs=[pl.BlockSpec((tm, w), row) for w in widths],
        out_shape=[jax.ShapeDtypeStruct((t, w), F32) for w in widths],
        compiler_params=pltpu.CompilerParams(
            dimension_semantics=("parallel",), vmem_limit_bytes=VMEM_LIMIT),
        name="in_proj",
    )(x, norm_w, w_cat, cosf, sinf)


def _ssd_prompt_kernel(xbc_ref, z_ref, dt_ref, cw_ref, cb_ref, dtb_ref, alog_ref, dsk_ref, nw_ref,
                       y_ref, hfin_ref, cfin_ref, h_scr, tail_scr, y_scr):
    c = pl.program_id(1)
    l = SSD_CHUNK

    @pl.when(c == 0)
    def _():
        h_scr[...] = jnp.zeros_like(h_scr)
        tail_scr[...] = jnp.zeros_like(tail_scr)

    xb = xbc_ref[...]
    xp = jnp.concatenate([tail_scr[...], xb], axis=0)
    conv = cb_ref[...]
    for i in range(CONV_WIDTH):
        lo = 8 - (CONV_WIDTH - 1) + i
        conv = conv + xp[lo:lo + l, :] * cw_ref[i:i + 1, :]
    tail_scr[...] = xb[l - 8:, :]
    u = _silu(conv)
    xs = u[:, :SSM_WIDTH]

    dt = _softplus(dt_ref[...] + dtb_ref[...])
    a_neg = -jnp.exp(alog_ref[...])
    row = lax.broadcasted_iota(jnp.int32, (l, l), 0)
    col = lax.broadcasted_iota(jnp.int32, (l, l), 1)
    causal = col <= row
    tri = jnp.where(causal, 1.0, 0.0).astype(BF16)
    hi, mid, lo3 = _split3(dt * a_neg)
    acum = _nn(tri, hi) + _nn(tri, mid) + _nn(tri, lo3)
    acum_t = acum.T
    a_last = acum[l - 1:l, :]
    dec_end = jnp.exp(a_last - acum)
    ea = jnp.exp(acum)
    chunk_decay = jnp.exp(a_last)

    for g in range(SSM_GROUPS):
        b0 = SSM_WIDTH + SSM_STATE * g
        c0 = SSM_WIDTH + SSM_GROUPS * SSM_STATE + SSM_STATE * g
        bb = u[:, b0:b0 + SSM_STATE].astype(BF16)
        cc = u[:, c0:c0 + SSM_STATE].astype(BF16)
        cb = _nt(cc, bb)
        for r in range(HEADS_PER_GROUP):
            h = g * HEADS_PER_GROUP + r
            p0 = SSM_HEADDIM * h
            seg = acum[:, h:h + 1] - acum_t[h:h + 1, :]
            lmat = jnp.exp(jnp.where(causal, seg, NEG))
            xh = xs[:, p0:p0 + SSM_HEADDIM]
            xdt = xh * dt[:, h:h + 1]
            hs = h_scr[h]
            y = _nn((cb * lmat).astype(BF16), xdt.astype(BF16))
            y = y + _nt(cc, hs.astype(BF16)) * ea[:, h:h + 1]
            y = y + dsk_ref[:, p0:p0 + SSM_HEADDIM] * xh
            st = _tn((xdt * dec_end[:, h:h + 1]).astype(BF16), bb)
            h_scr[h] = hs * chunk_decay[:, h:h + 1] + st
            y_scr[:, p0:p0 + SSM_HEADDIM] = y

    y_ref[...] = _rmsnorm(y_scr[...] * _silu(z_ref[...]), nw_ref[...])

    @pl.when(c == pl.num_programs(1) - 1)
    def _():
        hfin_ref[0] = h_scr[...]
        cfin_ref[0] = xb[l - (CONV_WIDTH - 1):, :]


def _ssd_prompt(xbc, z, dt, conv_w, conv_b, dtb, alog, dsk, nw, batch, seq):
    nc = seq // SSD_CHUNK
    row = lambda b, c: (b * nc + c, 0)
    const = lambda b, c: (0, 0)
    return pl.pallas_call(
        _ssd_prompt_kernel,
        grid=(batch, nc),
        in_specs=[
            pl.BlockSpec((SSD_CHUNK, XBC_WIDTH), row),
            pl.BlockSpec((SSD_CHUNK, SSM_WIDTH), row),
            pl.BlockSpec((SSD_CHUNK, DT_PAD), row),
            pl.BlockSpec((CONV_WIDTH, XBC_WIDTH), const),
            pl.BlockSpec((1, XBC_WIDTH), const),
            pl.BlockSpec((1, DT_PAD), const),
            pl.BlockSpec((1, DT_PAD), const),
            pl.BlockSpec((1, SSM_WIDTH), const),
            pl.BlockSpec((1, SSM_WIDTH), const),
        ],
        out_specs=[
            pl.BlockSpec((SSD_CHUNK, SSM_WIDTH), row),
            pl.BlockSpec((1, SSM_HEADS, SSM_HEADDIM, SSM_STATE), lambda b, c: (b, 0, 0, 0)),
            pl.BlockSpec((1, CONV_WIDTH - 1, XBC_WIDTH), lambda b, c: (b, 0, 0)),
        ],
        out_shape=[
            jax.ShapeDtypeStruct((batch * seq, SSM_WIDTH), F32),
            jax.ShapeDtypeStruct((batch, SSM_HEADS, SSM_HEADDIM, SSM_STATE), F32),
            jax.ShapeDtypeStruct((batch, CONV_WIDTH - 1, XBC_WIDTH), F32),
        ],
        scratch_shapes=[
            pltpu.VMEM((SSM_HEADS, SSM_HEADDIM, SSM_STATE), F32),
            pltpu.VMEM((8, XBC_WIDTH), F32),
            pltpu.VMEM((SSD_CHUNK, SSM_WIDTH), F32),
        ],
        compiler_params=pltpu.CompilerParams(
            dimension_semantics=("parallel", "arbitrary"), vmem_limit_bytes=VMEM_LIMIT),
        name="ssd_prompt",
    )(xbc, z, dt, conv_w, conv_b, dtb, alog, dsk, nw)


def _attn_prompt_kernel(q_ref, g_ref, k_ref, v_ref, o_ref, kb_scr, vt_scr, km_scr, st_scr, *, k_sel):
    n = pl.program_id(2)
    nblk = kb_scr.shape[0]

    @pl.when(n == 0)
    def _():
        for j in range(nblk):
            kf = k_ref[MOBA_BLOCK * j:MOBA_BLOCK * (j + 1), :]
            kb_scr[j] = kf.astype(BF16)
            km_scr[j:j + 1, :] = jnp.mean(kf, axis=0, keepdims=True)
            vt = v_ref[MOBA_BLOCK * j:MOBA_BLOCK * (j + 1), :].T.astype(BF16)
            vt_scr[j] = jnp.concatenate([vt, jnp.ones((SUM_ROWS, MOBA_BLOCK), BF16)], axis=0)

    own = (n * Q_TILE) // MOBA_BLOCK
    qf = q_ref[...]
    qt = jnp.concatenate([qf[:, LANES * r:LANES * (r + 1)].T for r in range(Q_PER_KV)], axis=1)
    qtb = (qt * SOFTMAX_C).astype(BF16)
    n_cols = Q_PER_KV * Q_TILE

    sels = []
    if k_sel > 0:
        kh, kmid, _ = _split3(km_scr[...])
        qh, qmid, _ = _split3(qt)
        gate = _nn(kh, qh) + _nn(kh, qmid) + _nn(kmid, qh)
        rowf = lax.broadcasted_iota(jnp.int32, gate.shape, 0).astype(F32)
        gate = jnp.where(rowf < own.astype(F32), gate, NEG)
        for t in range(k_sel):
            m = jnp.max(gate, axis=0, keepdims=True)
            idx = jnp.min(jnp.where(gate == m, rowf, float(nblk)), axis=0, keepdims=True)
            sels.append(jnp.where(t < own, idx, -1.0))
            gate = jnp.where(rowf == idx, -jnp.inf, gate)

    def scores(j):
        return _nn(kb_scr[j], qtb)

    def chose(j):
        jf = lax.convert_element_type(j, F32)
        hit = sels[0] == jf
        for t in range(1, k_sel):
            hit = jnp.logical_or(hit, sels[t] == jf)
        return hit

    kpos = own * MOBA_BLOCK + lax.broadcasted_iota(jnp.int32, (MOBA_BLOCK, n_cols), 0)
    qpos = n * Q_TILE + (lax.broadcasted_iota(jnp.int32, (MOBA_BLOCK, n_cols), 1) & (Q_TILE - 1))
    st = jnp.where(kpos <= qpos, scores(own), NEG)
    m = jnp.max(st, axis=0, keepdims=True)
    acc = _nn(vt_scr[own], jnp.exp2(st - m).astype(BF16))

    n_groups = (own + ATTN_GROUP - 1) // ATTN_GROUP

    def produce(grp):
        mg = jnp.full((1, n_cols), NEG, F32)
        for u in range(ATTN_GROUP):
            j = grp * ATTN_GROUP + u
            st = scores(jnp.minimum(j, nblk - 1))
            st_scr[grp % 2, u] = st
            mg = jnp.maximum(mg, jnp.where(chose(j), jnp.max(st, axis=0, keepdims=True), NEG))
        return mg

    if k_sel > 0:
        def body(grp, carry):
            m, acc, mg = carry
            m_new = jnp.maximum(m, mg)
            alpha = jnp.exp2(m - m_new)
            acc = alpha * acc
            for u in range(ATTN_GROUP):
                j = grp * ATTN_GROUP + u
                p = jnp.exp2(st_scr[grp % 2, u] - jnp.where(chose(j), m_new, -NEG))
                acc = acc + _nn(vt_scr[jnp.minimum(j, nblk - 1)], p.astype(BF16))
            mg_next = produce(grp + 1)
            return m_new, acc, mg_next
        _, acc, _ = lax.fori_loop(0, n_groups, body, (m, acc, produce(0)))

    o = acc[:ATTN_HEADDIM] / acc[ATTN_HEADDIM:ATTN_HEADDIM + 1]
    gg = _silu(g_ref[...])
    for r in range(Q_PER_KV):
        o_ref[:, LANES * r:LANES * (r + 1)] = (
            o[:, Q_TILE * r:Q_TILE * (r + 1)].T * gg[:, LANES * r:LANES * (r + 1)])


def _attn_prompt(q, g, k, v, batch, seq):
    nq = seq // Q_TILE
    nblk = seq // MOBA_BLOCK
    k_sel = min(MOBA_TOPK, nblk - 1)
    qmap = lambda b, h, n: (b * nq + n, h)
    kvmap = lambda b, h, n: (b, h)
    width = Q_PER_KV * ATTN_HEADDIM
    return pl.pallas_call(
        functools.partial(_attn_prompt_kernel, k_sel=k_sel),
        grid=(batch, KV_HEADS, nq),
        in_specs=[
            pl.BlockSpec((Q_TILE, width), qmap),
            pl.BlockSpec((Q_TILE, width), qmap),
            pl.BlockSpec((seq, ATTN_HEADDIM), kvmap),
            pl.BlockSpec((seq, ATTN_HEADDIM), kvmap),
        ],
        out_specs=pl.BlockSpec((Q_TILE, width), qmap),
        out_shape=jax.ShapeDtypeStruct((batch * seq, ATTN_WIDTH), F32),
        scratch_shapes=[
            pltpu.VMEM((nblk, MOBA_BLOCK, ATTN_HEADDIM), BF16),
            pltpu.VMEM((nblk, ATTN_HEADDIM + SUM_ROWS, MOBA_BLOCK), BF16),
            pltpu.VMEM((nblk, ATTN_HEADDIM), F32),
            pltpu.VMEM((2, ATTN_GROUP, MOBA_BLOCK, Q_PER_KV * Q_TILE), F32),
        ],
        compiler_params=pltpu.CompilerParams(
            dimension_semantics=("parallel", "parallel", "arbitrary"),
            vmem_limit_bytes=VMEM_LIMIT),
        name="moba_prompt",
    )(q, g, k, v)


def _out_kernel(ys_ref, ya_ref, x_ref, w1_ref, w2_ref, nw_ref, o_ref):
    out = x_ref[...] + _nn(ys_ref[...].astype(BF16), w1_ref[...])
    out = out + _nn(ya_ref[...].astype(BF16), w2_ref[...])
    o_ref[...] = _rmsnorm(out, nw_ref[...])


def _out_proj(ys, ya, x, w1, w2, nw, tm):
    t = x.shape[0]
    row = lambda i: (i, 0)
    const = lambda i: (0, 0)
    return pl.pallas_call(
        _out_kernel,
        grid=(t // tm,),
        in_specs=[
            pl.BlockSpec((tm, SSM_WIDTH), row),
            pl.BlockSpec((tm, ATTN_WIDTH), row),
            pl.BlockSpec((tm, D_MODEL), row),
            pl.BlockSpec((SSM_WIDTH, D_MODEL), const),
            pl.BlockSpec((ATTN_WIDTH, D_MODEL), const),
            pl.BlockSpec((1, D_MODEL), const),
        ],
        out_specs=pl.BlockSpec((tm, D_MODEL), row),
        out_shape=jax.ShapeDtypeStruct((t, D_MODEL), F32),
        compiler_params=pltpu.CompilerParams(
            dimension_semantics=("parallel",), vmem_limit_bytes=VMEM_LIMIT),
        name="out_proj",
    )(ys, ya, x, w1, w2, nw)


def _ssd_sample_kernel(xbc_ref, cst_ref, z_ref, dt_ref, h0_ref, cw_ref, cb_ref, dtb_ref, alog_ref,
                       dsk_ref, nw_ref, eh_ref, eg_ref,
                       y_ref, hnew_ref, cnew_ref, *, steps):
    tb = SAMPLE_TILE
    hist = CONV_WIDTH - 1
    rows = [cst_ref[:, j, :] for j in range(hist)] + [xbc_ref[:, t, :] for t in range(steps)]
    for j in range(hist):
        cnew_ref[:, j, :] = rows[steps + j]

    a_neg = -jnp.exp(alog_ref[...])
    u, dts, acs = [], [], []
    run = None
    for t in range(steps):
        conv = cb_ref[...]
        for i in range(CONV_WIDTH):
            conv = conv + rows[t + i] * cw_ref[i:i + 1, :]
        u.append(_silu(conv))
        dt = _softplus(dt_ref[:, t, :] + dtb_ref[...])
        dts.append(dt)
        run = dt * a_neg if run is None else run + dt * a_neg
        acs.append(run)
    uu = jnp.concatenate(u, axis=0)
    xs = uu[:, :SSM_WIDTH]
    bm = uu[:, SSM_WIDTH:SSM_WIDTH + SSM_GROUPS * SSM_STATE]
    cm = uu[:, SSM_WIDTH + SSM_GROUPS * SSM_STATE:]
    dt_all = jnp.concatenate(dts, axis=0)
    ac_all = jnp.concatenate(acs, axis=0)
    a_last = jnp.concatenate([acs[-1]] * steps, axis=0)

    eh = eh_ref[...]
    xdt = xs * _nn_exact_rhs(dt_all, eh)
    xw = xs * _nn_exact_rhs(dt_all * jnp.exp(a_last - ac_all), eh)
    ea = _nn_exact_rhs(jnp.exp(ac_all), eh)

    eg = eg_ref[...]
    ydiag = []
    for t in range(steps):
        acc = None
        for j in range(t + 1):
            cbx = _nn_exact_rhs(cm[tb * t:tb * (t + 1)] * bm[tb * j:tb * (j + 1)], eg)
            lmx = _nn_exact_rhs(jnp.exp(acs[t] - acs[j]), eh)
            term = cbx * lmx * xdt[tb * j:tb * (j + 1)]
            acc = term if acc is None else acc + term
        ydiag.append(acc)
    y = jnp.concatenate(ydiag, axis=0) + dsk_ref[...] * xs

    rowb = lax.broadcasted_iota(jnp.int32, (steps * tb, 1), 0) % tb
    cmb = cm.astype(BF16)
    gw = HEADS_PER_GROUP * SSM_HEADDIM
    yoff = [None] * SSM_GROUPS
    for b in range(tb):
        mine = rowb == b
        for g in range(SSM_GROUPS):
            h0g = h0_ref[b, HEADS_PER_GROUP * g:HEADS_PER_GROUP * (g + 1)].reshape(gw, SSM_STATE)
            part = jnp.where(mine, _nt(cmb[:, SSM_STATE * g:SSM_STATE * (g + 1)], h0g.astype(BF16)), 0.0)
            yoff[g] = part if yoff[g] is None else yoff[g] + part
            xwb = jnp.where(mine, xw[:, gw * g:gw * (g + 1)], 0.0)
            xh, xm, _ = _split3(xwb)
            bh, bmid, _ = _split3(bm[:, SSM_STATE * g:SSM_STATE * (g + 1)])
            st = _tn(xh, bh) + _tn(xh, bmid) + _tn(xm, bh)
            for r in range(HEADS_PER_GROUP):
                h = HEADS_PER_GROUP * g + r
                dec = jnp.exp(acs[-1][b:b + 1, h:h + 1])
                hnew_ref[b, h] = (h0g[SSM_HEADDIM * r:SSM_HEADDIM * (r + 1)] * dec
                                  + st[SSM_HEADDIM * r:SSM_HEADDIM * (r + 1)])
    y = y + jnp.concatenate(yoff, axis=1) * ea

    for t in range(steps):
        yt = y[tb * t:tb * (t + 1)] * _silu(z_ref[:, t, :])
        y_ref[:, t, :] = _rmsnorm(yt, nw_ref[...])


def _ssd_sample(xbc, cst, z, dt, h0, conv_w, conv_b, dtb, alog, dsk, nw, eh, eg):
    nb, steps = xbc.shape[0], xbc.shape[1]
    tb = SAMPLE_TILE
    c2 = lambda i: (0, 0)
    b3 = lambda i: (i, 0, 0)
    b4 = lambda i: (i, 0, 0, 0)
    hist = CONV_WIDTH - 1
    return pl.pallas_call(
        functools.partial(_ssd_sample_kernel, steps=steps),
        grid=(nb // tb,),
        in_specs=[
            pl.BlockSpec((tb, steps, XBC_WIDTH), b3),
            pl.BlockSpec((tb, hist, XBC_WIDTH), b3),
            pl.BlockSpec((tb, steps, SSM_WIDTH), b3),
            pl.BlockSpec((tb, steps, DT_PAD), b3),
            pl.BlockSpec((tb, SSM_HEADS, SSM_HEADDIM, SSM_STATE), b4),
            pl.BlockSpec((CONV_WIDTH, XBC_WIDTH), c2),
            pl.BlockSpec((1, XBC_WIDTH), c2),
            pl.BlockSpec((1, DT_PAD), c2),
            pl.BlockSpec((1, DT_PAD), c2),
            pl.BlockSpec((1, SSM_WIDTH), c2),
            pl.BlockSpec((1, SSM_WIDTH), c2),
            pl.BlockSpec((DT_PAD, SSM_WIDTH), c2),
            pl.BlockSpec((SSM_GROUPS * SSM_STATE, SSM_WIDTH), c2),
        ],
        out_specs=[
            pl.BlockSpec((tb, steps, SSM_WIDTH), b3),
            pl.BlockSpec((tb, SSM_HEADS, SSM_HEADDIM, SSM_STATE), b4),
            pl.BlockSpec((tb, hist, XBC_WIDTH), b3),
        ],
        out_shape=[
            jax.ShapeDtypeStruct((nb, steps, SSM_WIDTH), F32),
            jax.ShapeDtypeStruct((nb, SSM_HEADS, SSM_HEADDIM, SSM_STATE), F32),
            jax.ShapeDtypeStruct((nb, hist, XBC_WIDTH), F32),
        ],
        compiler_params=pltpu.CompilerParams(
            dimension_semantics=("parallel",), vmem_limit_bytes=VMEM_LIMIT),
        name="ssd_sample",
    )(xbc, cst, z, dt, h0, conv_w, conv_b, dtb, alog, dsk, nw, eh, eg)


def _attn_sample_kernel(pt_ref, q_ref, g_ref, kn_ref, vn_ref, ck_ref, cv_ref, o_ref,
                        kbuf, vbuf, km_scr, w_scr, idx_v, idx_s, o_scr, sems, *, n_pages, steps):
    b = pl.program_id(0)
    slot = b % 2
    n_full = n_pages // PAGES_PER_BLOCK
    page_rows = PAGE_SIZE * KV_HEADS
    n_cols = ATTN_HEADS * steps
    per_kv = Q_PER_KV * steps

    def k_copy(seq, sl, p):
        return pltpu.make_async_copy(ck_ref.at[pt_ref[seq * n_pages + p]], kbuf.at[sl, p], sems.at[sl])

    def v_copy(p):
        return pltpu.make_async_copy(cv_ref.at[pt_ref[b * n_pages + p]], vbuf.at[p], sems.at[2])

    @pl.when(b == 0)
    def _():
        for p in range(n_pages):
            k_copy(0, 0, p).start(priority=p % 2)

    for p in range(n_pages):
        v_copy(p).start(priority=p % 2)

    for p in range(n_pages):
        k_copy(b, slot, p).wait()

    km_scr[...] = jnp.zeros_like(km_scr)
    for blk in range(n_full):
        tot = None
        for pp in range(PAGES_PER_BLOCK):
            page = kbuf[slot, PAGES_PER_BLOCK * blk + pp]
            part = page.reshape(page_rows // 8, 8, LANES).sum(axis=0)
            tot = part if tot is None else tot + part
        km_scr[KV_HEADS * blk:KV_HEADS * (blk + 1), :] = (
            (tot[:KV_HEADS] + tot[KV_HEADS:]) * (1.0 / MOBA_BLOCK))

    qf = q_ref[0]
    q_all = jnp.concatenate([qf[:, LANES * h:LANES * (h + 1)] for h in range(ATTN_HEADS)], axis=0)
    q_pad = jnp.concatenate([q_all, jnp.zeros((LANES - n_cols, LANES), F32)], axis=0)

    gate = _nt_x3(q_pad, km_scr[...])
    lane = lax.broadcasted_iota(jnp.int32, gate.shape, 1)
    rowc = lax.broadcasted_iota(jnp.int32, gate.shape, 0)
    ok = jnp.logical_and(lane % KV_HEADS == rowc // per_kv, lane < KV_HEADS * n_full)
    gate = jnp.where(ok, gate, -jnp.inf)
    lanef = lane.astype(F32)
    chosen = jnp.zeros(gate.shape, jnp.int32)
    for t in range(MOBA_TOPK):
        m = jnp.max(gate, axis=-1, keepdims=True)
        idx = jnp.min(jnp.where(gate == m, lanef, float(4 * LANES)), axis=-1, keepdims=True)
        chosen = jnp.where(lane == t, idx.astype(jnp.int32) // KV_HEADS, chosen)
        gate = jnp.where(lanef == idx, -jnp.inf, gate)
    idx_v[...] = chosen
    to_smem = pltpu.make_async_copy(idx_v, idx_s, sems.at[3])
    to_smem.start()

    q_t = q_pad.T
    for c in range(n_cols):
        w_scr[c] = jnp.broadcast_to(q_t[:, c:c + 1], (ATTN_HEADDIM, LANES)).astype(BF16)

    to_smem.wait()
    for p in range(n_pages):
        v_copy(p).wait()

    @pl.when(b + 1 < pl.num_programs(0))
    def _():
        for p in range(n_pages):
            k_copy(b + 1, 1 - slot, p).start(priority=p % 2)

    sub8 = lax.broadcasted_iota(jnp.int32, (8, LANES), 0)

    def k_tile(blk, g):
        return jnp.concatenate(
            [kbuf[slot, PAGES_PER_BLOCK * blk + pp, pl.ds(g, PAGE_SIZE, stride=KV_HEADS), :]
             for pp in range(PAGES_PER_BLOCK)], axis=0)

    def v_tile(blk, g):
        return jnp.concatenate(
            [vbuf[PAGES_PER_BLOCK * blk + pp, pl.ds(g, PAGE_SIZE, stride=KV_HEADS), :]
             for pp in range(PAGES_PER_BLOCK)], axis=0)

    def column(c, i, g, knb, vn):
        w = w_scr[c]
        s_own = jnp.where(sub8 <= i % steps, _nn(knb, w) * SOFTMAX_C, NEG)
        mx = jnp.max(s_own, axis=0, keepdims=True)
        blks = [idx_s[c, t] for t in range(MOBA_TOPK)]
        scores = []
        for blk in blks:
            s = _nn(k_tile(blk, g).astype(BF16), w) * SOFTMAX_C
            scores.append(s)
            mx = jnp.maximum(mx, jnp.max(s, axis=0, keepdims=True))
        p = jnp.exp2(s_own - mx)
        l = jnp.sum(p, axis=0, keepdims=True)
        acc = jnp.sum(p * vn, axis=0, keepdims=True)
        for blk, s in zip(blks, scores):
            p = jnp.exp2(s - mx)
            l = l + jnp.sum(p, axis=0, keepdims=True)
            acc = acc + jnp.sum(p * v_tile(blk, g), axis=0, keepdims=True)
        return acc / l

    for g in range(KV_HEADS):
        pad = jnp.zeros((8 - steps, LANES), F32)
        kn = jnp.concatenate([kn_ref[0][:, LANES * g:LANES * (g + 1)], pad], axis=0)
        vn = jnp.concatenate([vn_ref[0][:, LANES * g:LANES * (g + 1)], pad], axis=0)
        knb = kn.astype(BF16)

        def pair_body(ip, _, g=g, knb=knb, vn=vn):
            for u in range(2):
                i = 2 * ip + u
                c = g * per_kv + i
                o_scr[pl.ds(c, 1), :] = column(c, i, g, knb, vn)
            return 0

        lax.fori_loop(0, per_kv // 2, pair_body, 0)

    gg = _silu(g_ref[0])
    for h in range(ATTN_HEADS):
        o_ref[0, :, LANES * h:LANES * (h + 1)] = (
            o_scr[steps * h:steps * (h + 1), :] * gg[:, LANES * h:LANES * (h + 1)])


def _attn_sample(page_table, q, g, k_new, v_new, cache_k, cache_v):
    nb, steps = q.shape[0], q.shape[1]
    n_pages = page_table.shape[1]
    n_pool = cache_k.shape[0]
    page_rows = PAGE_SIZE * KV_HEADS
    ck = cache_k.reshape(n_pool, page_rows, ATTN_HEADDIM)
    cv = cache_v.reshape(n_pool, page_rows, ATTN_HEADDIM)
    b3 = lambda i, pt: (i, 0, 0)
    grid_spec = pltpu.PrefetchScalarGridSpec(
        num_scalar_prefetch=1,
        grid=(nb,),
        in_specs=[
            pl.BlockSpec((1, steps, ATTN_WIDTH), b3),
            pl.BlockSpec((1, steps, ATTN_WIDTH), b3),
            pl.BlockSpec((1, steps, KV_WIDTH), b3),
            pl.BlockSpec((1, steps, KV_WIDTH), b3),
            pl.BlockSpec(memory_space=pl.ANY),
            pl.BlockSpec(memory_space=pl.ANY),
        ],
        out_specs=pl.BlockSpec((1, steps, ATTN_WIDTH), b3),
        scratch_shapes=[
            pltpu.VMEM((2, n_pages, page_rows, ATTN_HEADDIM), F32),
            pltpu.VMEM((n_pages, page_rows, ATTN_HEADDIM), F32),
            pltpu.VMEM((LANES, ATTN_HEADDIM), F32),
            pltpu.VMEM((ATTN_HEADS * steps, ATTN_HEADDIM, LANES), BF16),
            pltpu.VMEM((LANES, LANES), jnp.int32),
            pltpu.SMEM((LANES, LANES), jnp.int32),
            pltpu.VMEM((ATTN_HEADS * steps, ATTN_HEADDIM), F32),
            pltpu.SemaphoreType.DMA((4,)),
        ],
    )
    return pl.pallas_call(
        functools.partial(_attn_sample_kernel, n_pages=n_pages, steps=steps),
        grid_spec=grid_spec,
        out_shape=jax.ShapeDtypeStruct((nb, steps, ATTN_WIDTH), F32),
        compiler_params=pltpu.CompilerParams(
            dimension_semantics=("arbitrary",), vmem_limit_bytes=VMEM_LIMIT),
        name="moba_sample",
    )(page_table.reshape(-1), q, g, k_new, v_new, ck, cv)


def _rope_tables(pos):
    inv = 1.0 / (ROPE_THETA ** (jnp.arange(0, ROT_DIM, 2, dtype=F32) / ROT_DIM))
    ang = pos.astype(F32)[:, None] * inv[None, :]
    cos, sin = jnp.cos(ang), jnp.sin(ang)
    n = pos.shape[0]
    rest = LANES - ROT_DIM
    cosf = jnp.concatenate([cos, cos, jnp.ones((n, rest), F32)], axis=1)
    sinf = jnp.concatenate([-sin, sin, jnp.zeros((n, rest), F32)], axis=1)
    return cosf, sinf


def _row_tile(t, cap):
    tm = cap
    while t % tm:
        tm //= 2
    return tm


def kernel(x_prompt, x_sample, cache_k, cache_v, state_ssm, state_conv, page_table,
           norm_w, w_in, conv_w, conv_b, dt_bias, a_log, d_skip, ssm_norm_w, w_out, final_norm_w):
    depth = w_in.shape[0]
    assert depth == 1
    bp, seq, _ = x_prompt.shape
    nb, steps, _ = x_sample.shape
    n_pages = page_table.shape[1]
    past = n_pages * PAGE_SIZE
    assert seq % MOBA_BLOCK == 0 and past % MOBA_BLOCK == 0 and steps <= 8
    assert n_pages // PAGES_PER_BLOCK >= MOBA_TOPK and nb % SAMPLE_TILE == 0
    assert KV_HEADS * (n_pages // PAGES_PER_BLOCK) <= LANES

    w = w_in[0]
    c_dt = SSM_WIDTH + XBC_WIDTH
    w_cat = jnp.concatenate(
        [w[:, :c_dt], w[:, c_dt + SSM_HEADS:], w[:, c_dt:c_dt + SSM_HEADS],
         jnp.zeros((D_MODEL, DT_PAD - SSM_HEADS), F32)], axis=1).astype(BF16)
    nw = norm_w[0].reshape(1, D_MODEL)
    pad_h = lambda v: jnp.concatenate([v, jnp.zeros((DT_PAD - SSM_HEADS,), F32)]).reshape(1, DT_PAD)
    dtb, alog = pad_h(dt_bias[0]), pad_h(a_log[0])
    dsk = jnp.repeat(d_skip[0], SSM_HEADDIM).reshape(1, SSM_WIDTH)
    snw = ssm_norm_w[0].reshape(1, SSM_WIDTH)
    cb = conv_b[0].reshape(1, XBC_WIDTH)
    w1 = w_out[0][:SSM_WIDTH].astype(BF16)
    w2 = w_out[0][SSM_WIDTH:].astype(BF16)
    fnw = final_norm_w.reshape(1, D_MODEL)

    xp = x_prompt.reshape(bp * seq, D_MODEL)
    tm_p = _row_tile(seq, 256)
    cos_p, sin_p = _rope_tables(jnp.arange(seq))
    z, xbc, q, k, v, g, dt = _proj(xp, nw, w_cat, cos_p, sin_p, tm_p)
    y_ssm, h_p, c_p = _ssd_prompt(xbc, z, dt, conv_w[0], cb, dtb, alog, dsk, snw, bp, seq)
    y_attn = _attn_prompt(q, g, k, v, bp, seq)
    y_p = _out_proj(y_ssm, y_attn, xp, w1, w2, fnw, _row_tile(bp * seq, 512))

    ts = nb * steps
    xs = x_sample.reshape(ts, D_MODEL)
    tm_s = _row_tile(ts, 256)
    cos_s, sin_s = _rope_tables(past + jnp.arange(steps))
    cos_s, sin_s = jnp.tile(cos_s, (tm_s // steps, 1)), jnp.tile(sin_s, (tm_s // steps, 1))
    zs, xbcs, qs, ks, vs, gs, dts = _proj(xs, nw, w_cat, cos_s, sin_s, tm_s)
    heads = jnp.arange(DT_PAD)[:, None]
    lanes = jnp.arange(SSM_WIDTH)[None, :]
    eh = (lanes // SSM_HEADDIM == heads).astype(BF16)
    rows_g = jnp.arange(SSM_GROUPS * SSM_STATE)[:, None] // SSM_STATE
    eg = (lanes // (HEADS_PER_GROUP * SSM_HEADDIM) == rows_g).astype(BF16)
    r3 = lambda a: a.reshape(nb, steps, a.shape[-1])
    ys_ssm, h_s, c_s = _ssd_sample(r3(xbcs), state_conv[0], r3(zs), r3(dts), state_ssm[0],
                                   conv_w[0], cb, dtb, alog, dsk, snw, eh, eg)
    ys_attn = _attn_sample(page_table, r3(qs), r3(gs), r3(ks), r3(vs), cache_k[0], cache_v[0])
    y_s = _out_proj(ys_ssm.reshape(ts, SSM_WIDTH), ys_attn.reshape(ts, ATTN_WIDTH), xs, w1, w2, fnw,
                    _row_tile(ts, 512))

    kv5 = lambda a, n, s: a.reshape(1, n, s, KV_HEADS, ATTN_HEADDIM)
    return (y_p.reshape(bp, seq, D_MODEL), y_s.reshape(nb, steps, D_MODEL),
            kv5(k, bp, seq), kv5(v, bp, seq), kv5(ks, nb, steps), kv5(vs, nb, steps),
            h_p[None], c_p[None], h_s[None], c_s[None])
```

```python
import functools

import jax
import jax.numpy as jnp
from jax import lax
from jax.experimental import pallas as pl
from jax.experimental.pallas import tpu as pltpu

F32 = jnp.float32
BF16 = jnp.bfloat16

D_MODEL = 1024
SSM_WIDTH = 1024
SSM_HEADDIM = 64
SSM_HEADS = SSM_WIDTH // SSM_HEADDIM
SSM_GROUPS = 4
HEADS_PER_GROUP = SSM_HEADS // SSM_GROUPS
SSM_STATE = 128
CONV_WIDTH = 4
SSD_CHUNK = 128
XBC_WIDTH = SSM_WIDTH + 2 * SSM_GROUPS * SSM_STATE
ATTN_HEADDIM = 128
ATTN_HEADS = 8
KV_HEADS = 4
Q_PER_KV = ATTN_HEADS // KV_HEADS
ATTN_WIDTH = ATTN_HEADS * ATTN_HEADDIM
KV_WIDTH = KV_HEADS * ATTN_HEADDIM
MOBA_BLOCK = 256
MOBA_TOPK = 3
Q_TILE = 256
ROT_DIM = 32
ROPE_THETA = 500000.0
PAGE_SIZE = 128
PAGES_PER_BLOCK = MOBA_BLOCK // PAGE_SIZE
EPS = 1e-6
NEG = -1e30
SOFTMAX_C = ATTN_HEADDIM ** -0.5 * 1.4426950408889634
LANES = 128
DT_PAD = LANES
PROJ_WIDTH = SSM_WIDTH + XBC_WIDTH + ATTN_WIDTH + 2 * KV_WIDTH + ATTN_WIDTH + DT_PAD
VMEM_LIMIT = 56 * 1024 * 1024
SAMPLE_TILE = 8
ATTN_GROUP = 4
SUM_ROWS = 16


def _nt(a, b):
    return lax.dot_general(a, b, (((1,), (1,)), ((), ())), preferred_element_type=F32)


def _tn(a, b):
    return lax.dot_general(a, b, (((0,), (0,)), ((), ())), preferred_element_type=F32)


def _nn(a, b):
    return jnp.dot(a, b, preferred_element_type=F32)


def _split3(x):
    hi = x.astype(BF16)
    r1 = x - hi.astype(F32)
    mid = r1.astype(BF16)
    lo = (r1 - mid.astype(F32)).astype(BF16)
    return hi, mid, lo


def _nn_exact_rhs(x, e):
    hi, mid, lo = _split3(x)
    return _nn(hi, e) + _nn(mid, e) + _nn(lo, e)


def _expand2(x, e):
    hi = x.astype(BF16)
    mid = (x - hi.astype(F32)).astype(BF16)
    return _nn(hi, e) + _nn(mid, e)


def _nt_x3(a, b):
    ah, am, _ = _split3(a)
    bh, bm, _ = _split3(b)
    return _nt(ah, bh) + _nt(ah, bm) + _nt(am, bh)


def _silu(x):
    return (0.5 * x) * (1.0 + jnp.tanh(0.5 * x))


def _softplus(x):
    return jnp.maximum(x, 0.0) + jnp.log1p(jnp.exp(-jnp.abs(x)))


def _rmsnorm(x, w):
    ms = jnp.mean(x * x, axis=-1, keepdims=True)
    return x * lax.rsqrt(ms + EPS) * w


def _proj_kernel(x_ref, nw_ref, w_ref, cos_ref, sin_ref,
                 z_ref, xbc_ref, q_ref, k_ref, v_ref, g_ref, dt_ref, krows_ref, vrows_ref):
    h = _rmsnorm(x_ref[...], nw_ref[...]).astype(BF16)
    tm = x_ref.shape[0]

    def mm(a, b):
        return _nn(h, w_ref[:, a:b])

    o = 0
    z_ref[...] = mm(o, o + SSM_WIDTH); o += SSM_WIDTH
    xbc_ref[...] = mm(o, o + XBC_WIDTH); o += XBC_WIDTH
    q = mm(o, o + ATTN_WIDTH); o += ATTN_WIDTH
    k = mm(o, o + KV_WIDTH); o += KV_WIDTH
    v = mm(o, o + KV_WIDTH); o += KV_WIDTH
    v_ref[...] = v
    for hh in range(KV_HEADS):
        vrows_ref[pl.ds(hh, tm, stride=KV_HEADS), :] = v[:, LANES * hh:LANES * (hh + 1)]
    g_ref[...] = mm(o, o + ATTN_WIDTH); o += ATTN_WIDTH
    dt_ref[...] = mm(o, o + DT_PAD)

    cosf = cos_ref[...]
    sinf = sin_ref[...]
    first_half = lax.broadcasted_iota(jnp.int32, cosf.shape, 1) < ROT_DIM // 2

    def rope(t, n_heads, out_ref, rows_ref=None):
        for hh in range(n_heads):
            th = t[:, LANES * hh:LANES * (hh + 1)]
            partner = jnp.where(first_half,
                                pltpu.roll(th, LANES - ROT_DIM // 2, 1),
                                pltpu.roll(th, ROT_DIM // 2, 1))
            roped = th * cosf + partner * sinf
            out_ref[:, LANES * hh:LANES * (hh + 1)] = roped
            if rows_ref is not None:
                rows_ref[pl.ds(hh, tm, stride=n_heads), :] = roped

    rope(q, ATTN_HEADS, q_ref)
    rope(k, KV_HEADS, k_ref, krows_ref)


def _proj(x, norm_w, w_cat, cosf, sinf, tm):
    t = x.shape[0]
    n_tab = cosf.shape[0] // tm
    row = lambda i: (i, 0)
    const = lambda i: (0, 0)
    widths = (SSM_WIDTH, XBC_WIDTH, ATTN_WIDTH, KV_WIDTH, KV_WIDTH, ATTN_WIDTH, DT_PAD)
    return pl.pallas_call(
        _proj_kernel,
        grid=(t // tm,),
        in_specs=[
            pl.BlockSpec((tm, D_MODEL), row),
            pl.BlockSpec((1, D_MODEL), const),
            pl.BlockSpec((D_MODEL, PROJ_WIDTH), const, pipeline_mode=pl.Buffered(1)),
            pl.BlockSpec((tm, LANES), lambda i: (i % n_tab, 0)),
            pl.BlockSpec((tm, LANES), lambda i: (i % n_tab, 0)),
        ],
        out_specs=([pl.BlockSpec((tm, w), row) for w in widths]
                   + [pl.BlockSpec((tm * KV_HEADS, ATTN_HEADDIM), row)] * 2),
        out_shape=([jax.ShapeDtypeStruct((t, w), F32) for w in widths]
                   + [jax.ShapeDtypeStruct((t * KV_HEADS, ATTN_HEADDIM), F32)] * 2),
        compiler_params=pltpu.CompilerParams(
            dimension_semantics=("parallel",), vmem_limit_bytes=VMEM_LIMIT),
        name="in_proj",
    )(x, norm_w, w_cat, cosf, sinf)


def _ssd_prompt_kernel(xbc_ref, z_ref, dt_ref, cw_ref, cb_ref, dtb_ref, alog_ref, dsk_ref, nw_ref, eh_ref,
                       y_ref, hfin_ref, cfin_ref, h_scr, xp_scr, yd_scr, yo_scr):
    c = pl.program_id(1)
    l = SSD_CHUNK
    hist = CONV_WIDTH - 1

    @pl.when(c == 0)
    def _():
        h_scr[...] = jnp.zeros_like(h_scr)
        xp_scr[0:8, :] = jnp.zeros((8, XBC_WIDTH), F32)

    xp_scr[8:8 + l, :] = xbc_ref[...]
    conv = cb_ref[...]
    for i in range(CONV_WIDTH):
        conv = conv + xp_scr[8 - hist + i:8 - hist + i + l, :] * cw_ref[i:i + 1, :]
    xp_scr[0:8, :] = xp_scr[l:l + 8, :]
    u = _silu(conv)
    xs = u[:, :SSM_WIDTH]

    dt = _softplus(dt_ref[...] + dtb_ref[...])
    a_neg = -jnp.exp(alog_ref[...])
    row = lax.broadcasted_iota(jnp.int32, (l, l), 0)
    col = lax.broadcasted_iota(jnp.int32, (l, l), 1)
    causal = col <= row
    tri = jnp.where(causal, 1.0, 0.0).astype(BF16)
    hi, mid, lo3 = _split3(dt * a_neg)
    acum = _nn(tri, hi) + _nn(tri, mid) + _nn(tri, lo3)
    acum_t = acum.T
    a_last = acum[l - 1:l, :]
    chunk_decay = jnp.exp(a_last)

    eh = eh_ref[...]
    xdt = xs * _expand2(dt, eh)
    xw = (xdt * _expand2(jnp.exp(a_last - acum), eh)).astype(BF16)
    xdtb = xdt.astype(BF16)
    upper = lax.broadcasted_iota(jnp.int32, (l, LANES), 1) < SSM_HEADDIM
    top_rows = lax.broadcasted_iota(jnp.int32, (2 * SSM_HEADDIM, SSM_STATE), 0) < SSM_HEADDIM
    zero = jnp.zeros((l, LANES), BF16)

    for g in range(SSM_GROUPS):
        b0 = SSM_WIDTH + SSM_STATE * g
        c0 = SSM_WIDTH + SSM_GROUPS * SSM_STATE + SSM_STATE * g
        bb = u[:, b0:b0 + SSM_STATE].astype(BF16)
        cc = u[:, c0:c0 + SSM_STATE].astype(BF16)
        cb = _nt(cc, bb)
        for r in range(0, HEADS_PER_GROUP, 2):
            h = g * HEADS_PER_GROUP + r
            p0 = SSM_HEADDIM * h
            ms = []
            for hh in (h, h + 1):
                seg = acum[:, hh:hh + 1] - acum_t[hh:hh + 1, :]
                ms.append((cb * jnp.exp(jnp.where(causal, seg, NEG))).astype(BF16))
            x2 = xdtb[:, p0:p0 + LANES]
            x_blockdiag = jnp.concatenate([jnp.where(upper, x2, zero), jnp.where(upper, zero, x2)], axis=0)
            yd_scr[:, p0:p0 + LANES] = _nn(jnp.concatenate(ms, axis=1), x_blockdiag)
            hs = h_scr[h:h + 2].reshape(2 * SSM_HEADDIM, SSM_STATE)
            yo_scr[:, p0:p0 + LANES] = _nt(cc, hs.astype(BF16))
            st = _tn(xw[:, p0:p0 + LANES], bb)
            decay = jnp.where(top_rows, chunk_decay[:, h:h + 1], chunk_decay[:, h + 1:h + 2])
            h_scr[h:h + 2] = (hs * decay + st).reshape(2, SSM_HEADDIM, SSM_STATE)

    y = yd_scr[...] + yo_scr[...] * _expand2(jnp.exp(acum), eh) + dsk_ref[...] * xs
    y_ref[...] = _rmsnorm(y * _silu(z_ref[...]), nw_ref[...])

    @pl.when(c == pl.num_programs(1) - 1)
    def _():
        hfin_ref[0] = h_scr[...]
        cfin_ref[0] = xp_scr[8 + l - hist:8 + l, :]


def _ssd_prompt(xbc, z, dt, conv_w, conv_b, dtb, alog, dsk, nw, eh, batch, seq):
    nc = seq // SSD_CHUNK
    row = lambda b, c: (b * nc + c, 0)
    const = lambda b, c: (0, 0)
    return pl.pallas_call(
        _ssd_prompt_kernel,
        grid=(batch, nc),
        in_specs=[
            pl.BlockSpec((SSD_CHUNK, XBC_WIDTH), row),
            pl.BlockSpec((SSD_CHUNK, SSM_WIDTH), row),
            pl.BlockSpec((SSD_CHUNK, DT_PAD), row),
            pl.BlockSpec((CONV_WIDTH, XBC_WIDTH), const),
            pl.BlockSpec((1, XBC_WIDTH), const),
            pl.BlockSpec((1, DT_PAD), const),
            pl.BlockSpec((1, DT_PAD), const),
            pl.BlockSpec((1, SSM_WIDTH), const),
            pl.BlockSpec((1, SSM_WIDTH), const),
            pl.BlockSpec((DT_PAD, SSM_WIDTH), const),
        ],
        out_specs=[
            pl.BlockSpec((SSD_CHUNK, SSM_WIDTH), row),
            pl.BlockSpec((1, SSM_HEADS, SSM_HEADDIM, SSM_STATE), lambda b, c: (b, 0, 0, 0)),
            pl.BlockSpec((1, CONV_WIDTH - 1, XBC_WIDTH), lambda b, c: (b, 0, 0)),
        ],
        out_shape=[
            jax.ShapeDtypeStruct((batch * seq, SSM_WIDTH), F32),
            jax.ShapeDtypeStruct((batch, SSM_HEADS, SSM_HEADDIM, SSM_STATE), F32),
            jax.ShapeDtypeStruct((batch, CONV_WIDTH - 1, XBC_WIDTH), F32),
        ],
        scratch_shapes=[
            pltpu.VMEM((SSM_HEADS, SSM_HEADDIM, SSM_STATE), F32),
            pltpu.VMEM((8 + SSD_CHUNK + 8, XBC_WIDTH), F32),
            pltpu.VMEM((SSD_CHUNK, SSM_WIDTH), F32),
            pltpu.VMEM((SSD_CHUNK, SSM_WIDTH), F32),
        ],
        compiler_params=pltpu.CompilerParams(
            dimension_semantics=("parallel", "arbitrary"), vmem_limit_bytes=VMEM_LIMIT),
        name="ssd_prompt",
    )(xbc, z, dt, conv_w, conv_b, dtb, alog, dsk, nw, eh)


def _attn_prompt_kernel(q_ref, g_ref, k_ref, v_ref, o_ref, kb_scr, vt_scr, km_scr, st_scr, *, k_sel):
    n = pl.program_id(2)
    nblk = kb_scr.shape[0]

    @pl.when(n == 0)
    def _():
        for j in range(nblk):
            kf = k_ref[MOBA_BLOCK * j:MOBA_BLOCK * (j + 1), :]
            kb_scr[j] = kf.astype(BF16)
            km_scr[j:j + 1, :] = jnp.mean(kf, axis=0, keepdims=True)
            vt = v_ref[MOBA_BLOCK * j:MOBA_BLOCK * (j + 1), :].T.astype(BF16)
            vt_scr[j] = jnp.concatenate([vt, jnp.ones((SUM_ROWS, MOBA_BLOCK), BF16)], axis=0)

    own = (n * Q_TILE) // MOBA_BLOCK
    qf = q_ref[...]
    qt = jnp.concatenate([qf[:, LANES * r:LANES * (r + 1)].T for r in range(Q_PER_KV)], axis=1)
    qtb = (qt * SOFTMAX_C).astype(BF16)
    n_cols = Q_PER_KV * Q_TILE

    sels = []
    if k_sel > 0:
        kh, kmid, _ = _split3(km_scr[...])
        qh, qmid, _ = _split3(qt)
        gate = _nn(kh, qh) + _nn(kh, qmid) + _nn(kmid, qh)
        rowf = lax.broadcasted_iota(jnp.int32, gate.shape, 0).astype(F32)
        gate = jnp.where(rowf < own.astype(F32), gate, NEG)
        for t in range(k_sel):
            m = jnp.max(gate, axis=0, keepdims=True)
            idx = jnp.min(jnp.where(gate == m, rowf, float(nblk)), axis=0, keepdims=True)
            sels.append(jnp.where(t < own, idx, -1.0))
            gate = jnp.where(rowf == idx, -jnp.inf, gate)

    def scores(j):
        return _nn(kb_scr[j], qtb)

    def chose(j):
        jf = lax.convert_element_type(j, F32)
        hit = sels[0] == jf
        for t in range(1, k_sel):
            hit = jnp.logical_or(hit, sels[t] == jf)
        return hit

    kpos = own * MOBA_BLOCK + lax.broadcasted_iota(jnp.int32, (MOBA_BLOCK, n_cols), 0)
    qpos = n * Q_TILE + (lax.broadcasted_iota(jnp.int32, (MOBA_BLOCK, n_cols), 1) & (Q_TILE - 1))
    st = jnp.where(kpos <= qpos, scores(own), NEG)
    m = jnp.max(st, axis=0, keepdims=True)
    acc = _nn(vt_scr[own], jnp.exp2(st - m).astype(BF16))

    n_groups = (own + ATTN_GROUP - 1) // ATTN_GROUP

    def produce(grp):
        mg = jnp.full((1, n_cols), NEG, F32)
        for u in range(ATTN_GROUP):
            j = grp * ATTN_GROUP + u
            st = scores(jnp.minimum(j, nblk - 1))
            st_scr[grp % 2, u] = st
            mg = jnp.maximum(mg, jnp.where(chose(j), jnp.max(st, axis=0, keepdims=True), NEG))
        return mg

    if k_sel > 0:
        def body(grp, carry):
            m, acc, mg = carry
            m_new = jnp.maximum(m, mg)
            alpha = jnp.exp2(m - m_new)
            acc = alpha * acc
            for u in range(ATTN_GROUP):
                j = grp * ATTN_GROUP + u
                p = jnp.exp2(st_scr[grp % 2, u] - jnp.where(chose(j), m_new, -NEG))
                acc = acc + _nn(vt_scr[jnp.minimum(j, nblk - 1)], p.astype(BF16))
            mg_next = produce(grp + 1)
            return m_new, acc, mg_next
        _, acc, _ = lax.fori_loop(0, n_groups, body, (m, acc, produce(0)))

    o = acc[:ATTN_HEADDIM] / acc[ATTN_HEADDIM:ATTN_HEADDIM + 1]
    gg = _silu(g_ref[...])
    for r in range(Q_PER_KV):
        o_ref[:, LANES * r:LANES * (r + 1)] = (
            o[:, Q_TILE * r:Q_TILE * (r + 1)].T * gg[:, LANES * r:LANES * (r + 1)])


def _attn_prompt(q, g, k, v, batch, seq):
    nq = seq // Q_TILE
    nblk = seq // MOBA_BLOCK
    k_sel = min(MOBA_TOPK, nblk - 1)
    qmap = lambda b, h, n: (b * nq + n, h)
    kvmap = lambda b, h, n: (b, h)
    width = Q_PER_KV * ATTN_HEADDIM
    return pl.pallas_call(
        functools.partial(_attn_prompt_kernel, k_sel=k_sel),
        grid=(batch, KV_HEADS, nq),
        in_specs=[
            pl.BlockSpec((Q_TILE, width), qmap),
            pl.BlockSpec((Q_TILE, width), qmap),
            pl.BlockSpec((seq, ATTN_HEADDIM), kvmap),
            pl.BlockSpec((seq, ATTN_HEADDIM), kvmap),
        ],
        out_specs=pl.BlockSpec((Q_TILE, width), qmap),
        out_shape=jax.ShapeDtypeStruct((batch * seq, ATTN_WIDTH), F32),
        scratch_shapes=[
            pltpu.VMEM((nblk, MOBA_BLOCK, ATTN_HEADDIM), BF16),
            pltpu.VMEM((nblk, ATTN_HEADDIM + SUM_ROWS, MOBA_BLOCK), BF16),
            pltpu.VMEM((nblk, ATTN_HEADDIM), F32),
            pltpu.VMEM((2, ATTN_GROUP, MOBA_BLOCK, Q_PER_KV * Q_TILE), F32),
        ],
        compiler_params=pltpu.CompilerParams(
            dimension_semantics=("parallel", "parallel", "arbitrary"),
            vmem_limit_bytes=VMEM_LIMIT),
        name="moba_prompt",
    )(q, g, k, v)


def _out_kernel(ys_ref, ya_ref, x_ref, w1_ref, w2_ref, nw_ref, o_ref):
    out = x_ref[...] + _nn(ys_ref[...].astype(BF16), w1_ref[...])
    out = out + _nn(ya_ref[...].astype(BF16), w2_ref[...])
    o_ref[...] = _rmsnorm(out, nw_ref[...])


def _out_proj(ys, ya, x, w1, w2, nw, tm):
    t = x.shape[0]
    row = lambda i: (i, 0)
    const = lambda i: (0, 0)
    return pl.pallas_call(
        _out_kernel,
        grid=(t // tm,),
        in_specs=[
            pl.BlockSpec((tm, SSM_WIDTH), row),
            pl.BlockSpec((tm, ATTN_WIDTH), row),
            pl.BlockSpec((tm, D_MODEL), row),
            pl.BlockSpec((SSM_WIDTH, D_MODEL), const),
            pl.BlockSpec((ATTN_WIDTH, D_MODEL), const),
            pl.BlockSpec((1, D_MODEL), const),
        ],
        out_specs=pl.BlockSpec((tm, D_MODEL), row),
        out_shape=jax.ShapeDtypeStruct((t, D_MODEL), F32),
        compiler_params=pltpu.CompilerParams(
            dimension_semantics=("parallel",), vmem_limit_bytes=VMEM_LIMIT),
        name="out_proj",
    )(ys, ya, x, w1, w2, nw)


def _ssd_sample_kernel(xbc_ref, cst_ref, z_ref, dt_ref, h0_ref, cw_ref, cb_ref, dtb_ref, alog_ref,
                       dsk_ref, nw_ref, eh_ref, eg_ref,
                       y_ref, hnew_ref, cnew_ref, *, steps):
    tb = SAMPLE_TILE
    hist = CONV_WIDTH - 1
    rows = [cst_ref[:, j, :] for j in range(hist)] + [xbc_ref[:, t, :] for t in range(steps)]
    for j in range(hist):
        cnew_ref[:, j, :] = rows[steps + j]

    a_neg = -jnp.exp(alog_ref[...])
    u, dts, acs = [], [], []
    run = None
    for t in range(steps):
        conv = cb_ref[...]
        for i in range(CONV_WIDTH):
            conv = conv + rows[t + i] * cw_ref[i:i + 1, :]
        u.append(_silu(conv))
        dt = _softplus(dt_ref[:, t, :] + dtb_ref[...])
        dts.append(dt)
        run = dt * a_neg if run is None else run + dt * a_neg
        acs.append(run)
    uu = jnp.concatenate(u, axis=0)
    xs = uu[:, :SSM_WIDTH]
    bm = uu[:, SSM_WIDTH:SSM_WIDTH + SSM_GROUPS * SSM_STATE]
    cm = uu[:, SSM_WIDTH + SSM_GROUPS * SSM_STATE:]
    dt_all = jnp.concatenate(dts, axis=0)
    ac_all = jnp.concatenate(acs, axis=0)
    a_last = jnp.concatenate([acs[-1]] * steps, axis=0)

    eh = eh_ref[...]
    xdt = xs * _nn_exact_rhs(dt_all, eh)
    xw = xs * _nn_exact_rhs(dt_all * jnp.exp(a_last - ac_all), eh)
    ea = _nn_exact_rhs(jnp.exp(ac_all), eh)

    eg = eg_ref[...]
    ydiag = []
    for t in range(steps):
        acc = None
        for j in range(t + 1):
            cbx = _nn_exact_rhs(cm[tb * t:tb * (t + 1)] * bm[tb * j:tb * (j + 1)], eg)
            lmx = _nn_exact_rhs(jnp.exp(acs[t] - acs[j]), eh)
            term = cbx * lmx * xdt[tb * j:tb * (j + 1)]
            acc = term if acc is None else acc + term
        ydiag.append(acc)
    y = jnp.concatenate(ydiag, axis=0) + dsk_ref[...] * xs

    rowb = lax.broadcasted_iota(jnp.int32, (steps * tb, 1), 0) % tb
    cmb = cm.astype(BF16)
    gw = HEADS_PER_GROUP * SSM_HEADDIM
    yoff = [None] * SSM_GROUPS
    for b in range(tb):
        mine = rowb == b
        for g in range(SSM_GROUPS):
            h0g = h0_ref[b, HEADS_PER_GROUP * g:HEADS_PER_GROUP * (g + 1)].reshape(gw, SSM_STATE)
            part = jnp.where(mine, _nt(cmb[:, SSM_STATE * g:SSM_STATE * (g + 1)], h0g.astype(BF16)), 0.0)
            yoff[g] = part if yoff[g] is None else yoff[g] + part
            xwb = jnp.where(mine, xw[:, gw * g:gw * (g + 1)], 0.0)
            xh, xm, _ = _split3(xwb)
            bh, bmid, _ = _split3(bm[:, SSM_STATE * g:SSM_STATE * (g + 1)])
            st = _tn(xh, bh) + _tn(xh, bmid) + _tn(xm, bh)
            for r in range(HEADS_PER_GROUP):
                h = HEADS_PER_GROUP * g + r
                dec = jnp.exp(acs[-1][b:b + 1, h:h + 1])
                hnew_ref[b, h] = (h0g[SSM_HEADDIM * r:SSM_HEADDIM * (r + 1)] * dec
                                  + st[SSM_HEADDIM * r:SSM_HEADDIM * (r + 1)])
    y = y + jnp.concatenate(yoff, axis=1) * ea

    for t in range(steps):
        yt = y[tb * t:tb * (t + 1)] * _silu(z_ref[:, t, :])
        y_ref[:, t, :] = _rmsnorm(yt, nw_ref[...])


def _ssd_sample(xbc, cst, z, dt, h0, conv_w, conv_b, dtb, alog, dsk, nw, eh, eg):
    nb, steps = xbc.shape[0], xbc.shape[1]
    tb = SAMPLE_TILE
    c2 = lambda i: (0, 0)
    b3 = lambda i: (i, 0, 0)
    b4 = lambda i: (i, 0, 0, 0)
    hist = CONV_WIDTH - 1
    return pl.pallas_call(
        functools.partial(_ssd_sample_kernel, steps=steps),
        grid=(nb // tb,),
        in_specs=[
            pl.BlockSpec((tb, steps, XBC_WIDTH), b3),
            pl.BlockSpec((tb, hist, XBC_WIDTH), b3),
            pl.BlockSpec((tb, steps, SSM_WIDTH), b3),
            pl.BlockSpec((tb, steps, DT_PAD), b3),
            pl.BlockSpec((tb, SSM_HEADS, SSM_HEADDIM, SSM_STATE), b4),
            pl.BlockSpec((CONV_WIDTH, XBC_WIDTH), c2),
            pl.BlockSpec((1, XBC_WIDTH), c2),
            pl.BlockSpec((1, DT_PAD), c2),
            pl.BlockSpec((1, DT_PAD), c2),
            pl.BlockSpec((1, SSM_WIDTH), c2),
            pl.BlockSpec((1, SSM_WIDTH), c2),
            pl.BlockSpec((DT_PAD, SSM_WIDTH), c2),
            pl.BlockSpec((SSM_GROUPS * SSM_STATE, SSM_WIDTH), c2),
        ],
        out_specs=[
            pl.BlockSpec((tb, steps, SSM_WIDTH), b3),
            pl.BlockSpec((tb, SSM_HEADS, SSM_HEADDIM, SSM_STATE), b4),
            pl.BlockSpec((tb, hist, XBC_WIDTH), b3),
        ],
        out_shape=[
            jax.ShapeDtypeStruct((nb, steps, SSM_WIDTH), F32),
            jax.ShapeDtypeStruct((nb, SSM_HEADS, SSM_HEADDIM, SSM_STATE), F32),
            jax.ShapeDtypeStruct((nb, hist, XBC_WIDTH), F32),
        ],
        compiler_params=pltpu.CompilerParams(
            dimension_semantics=("parallel",), vmem_limit_bytes=VMEM_LIMIT),
        name="ssd_sample",
    )(xbc, cst, z, dt, h0, conv_w, conv_b, dtb, alog, dsk, nw, eh, eg)


def _attn_sample_kernel(pt_ref, q_ref, g_ref, kn_ref, vn_ref, ck_ref, cv_ref, o_ref,
                        kbuf, vbuf, km_scr, w_scr, idx_v, idx_s, o_scr, sems, *, n_pages, steps):
    b = pl.program_id(0)
    slot = b % 2
    n_full = n_pages // PAGES_PER_BLOCK
    page_rows = PAGE_SIZE * KV_HEADS
    n_cols = ATTN_HEADS * steps
    per_kv = Q_PER_KV * steps

    def k_copy(seq, sl, p):
        return pltpu.make_async_copy(ck_ref.at[pt_ref[seq * n_pages + p]], kbuf.at[sl, p], sems.at[sl])

    def v_copy(p):
        return pltpu.make_async_copy(cv_ref.at[pt_ref[b * n_pages + p]], vbuf.at[p], sems.at[2])

    @pl.when(b == 0)
    def _():
        for p in range(n_pages):
            k_copy(0, 0, p).start(priority=p % 2)

    for p in range(n_pages):
        v_copy(p).start(priority=p % 2)

    for p in range(n_pages):
        k_copy(b, slot, p).wait()

    km_scr[...] = jnp.zeros_like(km_scr)
    for blk in range(n_full):
        tot = None
        for pp in range(PAGES_PER_BLOCK):
            page = kbuf[slot, PAGES_PER_BLOCK * blk + pp]
            part = page.reshape(page_rows // 8, 8, LANES).sum(axis=0)
            tot = part if tot is None else tot + part
        km_scr[KV_HEADS * blk:KV_HEADS * (blk + 1), :] = (
            (tot[:KV_HEADS] + tot[KV_HEADS:]) * (1.0 / MOBA_BLOCK))

    qf = q_ref[0]
    q_all = jnp.concatenate([qf[:, LANES * h:LANES * (h + 1)] for h in range(ATTN_HEADS)], axis=0)
    q_pad = jnp.concatenate([q_all, jnp.zeros((LANES - n_cols, LANES), F32)], axis=0)

    gate = _nt_x3(q_pad, km_scr[...])
    lane = lax.broadcasted_iota(jnp.int32, gate.shape, 1)
    rowc = lax.broadcasted_iota(jnp.int32, gate.shape, 0)
    ok = jnp.logical_and(lane % KV_HEADS == rowc // per_kv, lane < KV_HEADS * n_full)
    gate = jnp.where(ok, gate, -jnp.inf)
    lanef = lane.astype(F32)
    chosen = jnp.zeros(gate.shape, jnp.int32)
    for t in range(MOBA_TOPK):
        m = jnp.max(gate, axis=-1, keepdims=True)
        idx = jnp.min(jnp.where(gate == m, lanef, float(4 * LANES)), axis=-1, keepdims=True)
        chosen = jnp.where(lane == t, idx.astype(jnp.int32) // KV_HEADS, chosen)
        gate = jnp.where(lanef == idx, -jnp.inf, gate)
    idx_v[...] = chosen
    to_smem = pltpu.make_async_copy(idx_v, idx_s, sems.at[3])
    to_smem.start()

    q_t = q_pad.T
    for c in range(n_cols):
        w_scr[c] = jnp.broadcast_to(q_t[:, c:c + 1], (ATTN_HEADDIM, LANES)).astype(BF16)

    to_smem.wait()
    for p in range(n_pages):
        v_copy(p).wait()

    @pl.when(b + 1 < pl.num_programs(0))
    def _():
        for p in range(n_pages):
            k_copy(b + 1, 1 - slot, p).start(priority=p % 2)

    sub8 = lax.broadcasted_iota(jnp.int32, (8, LANES), 0)

    def k_tile(blk, g):
        return jnp.concatenate(
            [kbuf[slot, PAGES_PER_BLOCK * blk + pp, pl.ds(g, PAGE_SIZE, stride=KV_HEADS), :]
             for pp in range(PAGES_PER_BLOCK)], axis=0)

    def v_tile(blk, g):
        return jnp.concatenate(
            [vbuf[PAGES_PER_BLOCK * blk + pp, pl.ds(g, PAGE_SIZE, stride=KV_HEADS), :]
             for pp in range(PAGES_PER_BLOCK)], axis=0)

    def column(c, i, g, knb, vn):
        w = w_scr[c]
        s_own = jnp.where(sub8 <= i % steps, _nn(knb, w) * SOFTMAX_C, NEG)
        mx = jnp.max(s_own, axis=0, keepdims=True)
        blks = [idx_s[c, t] for t in range(MOBA_TOPK)]
        scores = []
        for blk in blks:
            s = _nn(k_tile(blk, g).astype(BF16), w) * SOFTMAX_C
            scores.append(s)
            mx = jnp.maximum(mx, jnp.max(s, axis=0, keepdims=True))
        p = jnp.exp2(s_own - mx)
        l = jnp.sum(p, axis=0, keepdims=True)
        acc = jnp.sum(p * vn, axis=0, keepdims=True)
        for blk, s in zip(blks, scores):
            p = jnp.exp2(s - mx)
            l = l + jnp.sum(p, axis=0, keepdims=True)
            acc = acc + jnp.sum(p * v_tile(blk, g), axis=0, keepdims=True)
        return acc / l

    for g in range(KV_HEADS):
        pad = jnp.zeros((8 - steps, LANES), F32)
        kn = jnp.concatenate([kn_ref[0][:, LANES * g:LANES * (g + 1)], pad], axis=0)
        vn = jnp.concatenate([vn_ref[0][:, LANES * g:LANES * (g + 1)], pad], axis=0)
        knb = kn.astype(BF16)

        def pair_body(ip, _, g=g, knb=knb, vn=vn):
            for u in range(2):
                i = 2 * ip + u
                c = g * per_kv + i
                o_scr[pl.ds(c, 1), :] = column(c, i, g, knb, vn)
            return 0

        lax.fori_loop(0, per_kv // 2, pair_body, 0)

    gg = _silu(g_ref[0])
    for h in range(ATTN_HEADS):
        o_ref[0, :, LANES * h:LANES * (h + 1)] = (
            o_scr[steps * h:steps * (h + 1), :] * gg[:, LANES * h:LANES * (h + 1)])


def _attn_sample(page_table, q, g, k_new, v_new, cache_k, cache_v):
    nb, steps = q.shape[0], q.shape[1]
    n_pages = page_table.shape[1]
    n_pool = cache_k.shape[0]
    page_rows = PAGE_SIZE * KV_HEADS
    ck = cache_k.reshape(n_pool, page_rows, ATTN_HEADDIM)
    cv = cache_v.reshape(n_pool, page_rows, ATTN_HEADDIM)
    b3 = lambda i, pt: (i, 0, 0)
    grid_spec = pltpu.PrefetchScalarGridSpec(
        num_scalar_prefetch=1,
        grid=(nb,),
        in_specs=[
            pl.BlockSpec((1, steps, ATTN_WIDTH), b3),
            pl.BlockSpec((1, steps, ATTN_WIDTH), b3),
            pl.BlockSpec((1, steps, KV_WIDTH), b3),
            pl.BlockSpec((1, steps, KV_WIDTH), b3),
            pl.BlockSpec(memory_space=pl.ANY),
            pl.BlockSpec(memory_space=pl.ANY),
        ],
        out_specs=pl.BlockSpec((1, steps, ATTN_WIDTH), b3),
        scratch_shapes=[
            pltpu.VMEM((2, n_pages, page_rows, ATTN_HEADDIM), F32),
            pltpu.VMEM((n_pages, page_rows, ATTN_HEADDIM), F32),
            pltpu.VMEM((LANES, ATTN_HEADDIM), F32),
            pltpu.VMEM((ATTN_HEADS * steps, ATTN_HEADDIM, LANES), BF16),
            pltpu.VMEM((LANES, LANES), jnp.int32),
            pltpu.SMEM((LANES, LANES), jnp.int32),
            pltpu.VMEM((ATTN_HEADS * steps, ATTN_HEADDIM), F32),
            pltpu.SemaphoreType.DMA((4,)),
        ],
    )
    return pl.pallas_call(
        functools.partial(_attn_sample_kernel, n_pages=n_pages, steps=steps),
        grid_spec=grid_spec,
        out_shape=jax.ShapeDtypeStruct((nb, steps, ATTN_WIDTH), F32),
        compiler_params=pltpu.CompilerParams(
            dimension_semantics=("arbitrary",), vmem_limit_bytes=VMEM_LIMIT),
        name="moba_sample",
    )(page_table.reshape(-1), q, g, k_new, v_new, ck, cv)


def _rope_tables(pos):
    inv = 1.0 / (ROPE_THETA ** (jnp.arange(0, ROT_DIM, 2, dtype=F32) / ROT_DIM))
    ang = pos.astype(F32)[:, None] * inv[None, :]
    cos, sin = jnp.cos(ang), jnp.sin(ang)
    n = pos.shape[0]
    rest = LANES - ROT_DIM
    cosf = jnp.concatenate([cos, cos, jnp.ones((n, rest), F32)], axis=1)
    sinf = jnp.concatenate([-sin, sin, jnp.zeros((n, rest), F32)], axis=1)
    return cosf, sinf


def _row_tile(t, cap):
    tm = cap
    while t % tm:
        tm //= 2
    return tm


def kernel(x_prompt, x_sample, cache_k, cache_v, state_ssm, state_conv, page_table,
           norm_w, w_in, conv_w, conv_b, dt_bias, a_log, d_skip, ssm_norm_w, w_out, final_norm_w):
    depth = w_in.shape[0]
    assert depth == 1
    bp, seq, _ = x_prompt.shape
    nb, steps, _ = x_sample.shape
    n_pages = page_table.shape[1]
    past = n_pages * PAGE_SIZE
    assert seq % MOBA_BLOCK == 0 and past % MOBA_BLOCK == 0 and steps <= 8
    assert n_pages // PAGES_PER_BLOCK >= MOBA_TOPK and nb % SAMPLE_TILE == 0
    assert KV_HEADS * (n_pages // PAGES_PER_BLOCK) <= LANES

    w = w_in[0]
    c_dt = SSM_WIDTH + XBC_WIDTH
    w_cat = jnp.concatenate(
        [w[:, :c_dt], w[:, c_dt + SSM_HEADS:], w[:, c_dt:c_dt + SSM_HEADS],
         jnp.zeros((D_MODEL, DT_PAD - SSM_HEADS), F32)], axis=1).astype(BF16)
    nw = norm_w[0].reshape(1, D_MODEL)
    pad_h = lambda v: jnp.concatenate([v, jnp.zeros((DT_PAD - SSM_HEADS,), F32)]).reshape(1, DT_PAD)
    dtb, alog = pad_h(dt_bias[0]), pad_h(a_log[0])
    dsk = jnp.repeat(d_skip[0], SSM_HEADDIM).reshape(1, SSM_WIDTH)
    snw = ssm_norm_w[0].reshape(1, SSM_WIDTH)
    cb = conv_b[0].reshape(1, XBC_WIDTH)
    w1 = w_out[0][:SSM_WIDTH].astype(BF16)
    w2 = w_out[0][SSM_WIDTH:].astype(BF16)
    fnw = final_norm_w.reshape(1, D_MODEL)

    heads = jnp.arange(DT_PAD)[:, None]
    lanes = jnp.arange(SSM_WIDTH)[None, :]
    eh = (lanes // SSM_HEADDIM == heads).astype(BF16)
    rows_g = jnp.arange(SSM_GROUPS * SSM_STATE)[:, None] // SSM_STATE
    eg = (lanes // (HEADS_PER_GROUP * SSM_HEADDIM) == rows_g).astype(BF16)

    xp = x_prompt.reshape(bp * seq, D_MODEL)
    tm_p = _row_tile(seq, 256)
    cos_p, sin_p = _rope_tables(jnp.arange(seq))
    z, xbc, q, k, v, g, dt, k_rows, v_rows = _proj(xp, nw, w_cat, cos_p, sin_p, tm_p)
    y_ssm, h_p, c_p = _ssd_prompt(xbc, z, dt, conv_w[0], cb, dtb, alog, dsk, snw, eh, bp, seq)
    y_attn = _attn_prompt(q, g, k, v, bp, seq)
    y_p = _out_proj(y_ssm, y_attn, xp, w1, w2, fnw, _row_tile(bp * seq, 512))

    ts = nb * steps
    xs = x_sample.reshape(ts, D_MODEL)
    tm_s = _row_tile(ts, 256)
    cos_s, sin_s = _rope_tables(past + jnp.arange(steps))
    cos_s, sin_s = jnp.tile(cos_s, (tm_s // steps, 1)), jnp.tile(sin_s, (tm_s // steps, 1))
    zs, xbcs, qs, ks, vs, gs, dts, ks_rows, vs_rows = _proj(xs, nw, w_cat, cos_s, sin_s, tm_s)
    r3 = lambda a: a.reshape(nb, steps, a.shape[-1])
    ys_ssm, h_s, c_s = _ssd_sample(r3(xbcs), state_conv[0], r3(zs), r3(dts), state_ssm[0],
                                   conv_w[0], cb, dtb, alog, dsk, snw, eh, eg)
    ys_attn = _attn_sample(page_table, r3(qs), r3(gs), r3(ks), r3(vs), cache_k[0], cache_v[0])
    y_s = _out_proj(ys_ssm.reshape(ts, SSM_WIDTH), ys_attn.reshape(ts, ATTN_WIDTH), xs, w1, w2, fnw,
                    _row_tile(ts, 512))

    kv5 = lambda a, n, s: a.reshape(1, n, s, KV_HEADS, ATTN_HEADDIM)
    return (y_p.reshape(bp, seq, D_MODEL), y_s.reshape(nb, steps, D_MODEL),
            kv5(k_rows, bp, seq), kv5(v_rows, bp, seq), kv5(ks_rows, nb, steps), kv5(vs_rows, nb, steps),
            h_p[None], c_p[None], h_s[None], c_s[None])
```

```python
import functools

import jax
import jax.numpy as jnp
from jax import lax
from jax.experimental import pallas as pl
from jax.experimental.pallas import tpu as pltpu

F32 = jnp.float32
BF16 = jnp.bfloat16

D_MODEL = 1024
SSM_WIDTH = 1024
SSM_HEADDIM = 64
SSM_HEADS = SSM_WIDTH // SSM_HEADDIM
SSM_GROUPS = 4
HEADS_PER_GROUP = SSM_HEADS // SSM_GROUPS
SSM_STATE = 128
CONV_WIDTH = 4
SSD_CHUNK = 128
XBC_WIDTH = SSM_WIDTH + 2 * SSM_GROUPS * SSM_STATE
ATTN_HEADDIM = 128
ATTN_HEADS = 8
KV_HEADS = 4
Q_PER_KV = ATTN_HEADS // KV_HEADS
ATTN_WIDTH = ATTN_HEADS * ATTN_HEADDIM
KV_WIDTH = KV_HEADS * ATTN_HEADDIM
MOBA_BLOCK = 256
MOBA_TOPK = 3
Q_TILE = 256
ROT_DIM = 32
ROPE_THETA = 500000.0
PAGE_SIZE = 128
PAGES_PER_BLOCK = MOBA_BLOCK // PAGE_SIZE
EPS = 1e-6
NEG = -1e30
SOFTMAX_C = ATTN_HEADDIM ** -0.5 * 1.4426950408889634
LANES = 128
DT_PAD = LANES
PROJ_WIDTH = SSM_WIDTH + XBC_WIDTH + ATTN_WIDTH + 2 * KV_WIDTH + ATTN_WIDTH + DT_PAD
VMEM_LIMIT = 56 * 1024 * 1024
SAMPLE_TILE = 8
ATTN_GROUP = 4
SUM_ROWS = 16


def _nt(a, b):
    return lax.dot_general(a, b, (((1,), (1,)), ((), ())), preferred_element_type=F32)


def _tn(a, b):
    return lax.dot_general(a, b, (((0,), (0,)), ((), ())), preferred_element_type=F32)


def _nn(a, b):
    return jnp.dot(a, b, preferred_element_type=F32)


def _split3(x):
    hi = x.astype(BF16)
    r1 = x - hi.astype(F32)
    mid = r1.astype(BF16)
    lo = (r1 - mid.astype(F32)).astype(BF16)
    return hi, mid, lo


def _nn_exact_rhs(x, e):
    hi, mid, lo = _split3(x)
    return _nn(hi, e) + _nn(mid, e) + _nn(lo, e)


def _expand2(x, e):
    hi = x.astype(BF16)
    mid = (x - hi.astype(F32)).astype(BF16)
    return _nn(hi, e) + _nn(mid, e)


def _nt_x3(a, b):
    ah, am, _ = _split3(a)
    bh, bm, _ = _split3(b)
    return _nt(ah, bh) + _nt(ah, bm) + _nt(am, bh)


def _silu(x):
    return (0.5 * x) * (1.0 + jnp.tanh(0.5 * x))


def _softplus(x):
    return jnp.maximum(x, 0.0) + jnp.log1p(jnp.exp(-jnp.abs(x)))


def _rmsnorm(x, w):
    ms = jnp.mean(x * x, axis=-1, keepdims=True)
    return x * lax.rsqrt(ms + EPS) * w


def _proj_kernel(x_ref, nw_ref, w_ref, cos_ref, sin_ref,
                 z_ref, xbc_ref, q_ref, k_ref, v_ref, g_ref, dt_ref, krows_ref, vrows_ref):
    h = _rmsnorm(x_ref[...], nw_ref[...]).astype(BF16)
    tm = x_ref.shape[0]

    def mm(a, b):
        return _nn(h, w_ref[:, a:b])

    o = 0
    z_ref[...] = mm(o, o + SSM_WIDTH); o += SSM_WIDTH
    xbc_ref[...] = mm(o, o + XBC_WIDTH); o += XBC_WIDTH
    q = mm(o, o + ATTN_WIDTH); o += ATTN_WIDTH
    k = mm(o, o + KV_WIDTH); o += KV_WIDTH
    v = mm(o, o + KV_WIDTH); o += KV_WIDTH
    v_ref[...] = v
    for hh in range(KV_HEADS):
        vrows_ref[pl.ds(hh, tm, stride=KV_HEADS), :] = v[:, LANES * hh:LANES * (hh + 1)]
    g_ref[...] = mm(o, o + ATTN_WIDTH); o += ATTN_WIDTH
    dt_ref[...] = mm(o, o + DT_PAD)

    cosf = cos_ref[...]
    sinf = sin_ref[...]
    first_half = lax.broadcasted_iota(jnp.int32, cosf.shape, 1) < ROT_DIM // 2

    def rope(t, n_heads, out_ref, rows_ref=None):
        for hh in range(n_heads):
            th = t[:, LANES * hh:LANES * (hh + 1)]
            partner = jnp.where(first_half,
                                pltpu.roll(th, LANES - ROT_DIM // 2, 1),
                                pltpu.roll(th, ROT_DIM // 2, 1))
            roped = th * cosf + partner * sinf
            out_ref[:, LANES * hh:LANES * (hh + 1)] = roped
            if rows_ref is not None:
                rows_ref[pl.ds(hh, tm, stride=n_heads), :] = roped

    rope(q, ATTN_HEADS, q_ref)
    rope(k, KV_HEADS, k_ref, krows_ref)


def _proj(x, norm_w, w_cat, cosf, sinf, tm):
    t = x.shape[0]
    n_tab = cosf.shape[0] // tm
    row = lambda i: (i, 0)
    const = lambda i: (0, 0)
    widths = (SSM_WIDTH, XBC_WIDTH, ATTN_WIDTH, KV_WIDTH, KV_WIDTH, ATTN_WIDTH, DT_PAD)
    return pl.pallas_call(
        _proj_kernel,
        grid=(t // tm,),
        in_specs=[
            pl.BlockSpec((tm, D_MODEL), row),
            pl.BlockSpec((1, D_MODEL), const),
            pl.BlockSpec((D_MODEL, PROJ_WIDTH), const, pipeline_mode=pl.Buffered(1)),
            pl.BlockSpec((tm, LANES), lambda i: (i % n_tab, 0)),
            pl.BlockSpec((tm, LANES), lambda i: (i % n_tab, 0)),
        ],
        out_specs=([pl.BlockSpec((tm, w), row) for w in widths]
                   + [pl.BlockSpec((tm * KV_HEADS, ATTN_HEADDIM), row)] * 2),
        out_shape=([jax.ShapeDtypeStruct((t, w), F32) for w in widths]
                   + [jax.ShapeDtypeStruct((t * KV_HEADS, ATTN_HEADDIM), F32)] * 2),
        compiler_params=pltpu.CompilerParams(
            dimension_semantics=("parallel",), vmem_limit_bytes=VMEM_LIMIT),
        name="in_proj",
    )(x, norm_w, w_cat, cosf, sinf)


def _ssd_prompt_kernel(xbc_ref, z_ref, dt_ref, cw_ref, cb_ref, dtb_ref, alog_ref, dsk_ref, nw_ref, eh_ref,
                       y_ref, hfin_ref, cfin_ref, h_scr, xp_scr, yd_scr, yo_scr):
    c = pl.program_id(1)
    l = SSD_CHUNK
    hist = CONV_WIDTH - 1

    @pl.when(c == 0)
    def _():
        h_scr[...] = jnp.zeros_like(h_scr)
        xp_scr[0:8, :] = jnp.zeros((8, XBC_WIDTH), F32)

    xp_scr[8:8 + l, :] = xbc_ref[...]
    conv = cb_ref[...]
    for i in range(CONV_WIDTH):
        conv = conv + xp_scr[8 - hist + i:8 - hist + i + l, :] * cw_ref[i:i + 1, :]
    xp_scr[0:8, :] = xp_scr[l:l + 8, :]
    u = _silu(conv)
    xs = u[:, :SSM_WIDTH]

    dt = _softplus(dt_ref[...] + dtb_ref[...])
    a_neg = -jnp.exp(alog_ref[...])
    row = lax.broadcasted_iota(jnp.int32, (l, l), 0)
    col = lax.broadcasted_iota(jnp.int32, (l, l), 1)
    causal = col <= row
    tri = jnp.where(causal, 1.0, 0.0).astype(BF16)
    hi, mid, lo3 = _split3(dt * a_neg)
    acum = _nn(tri, hi) + _nn(tri, mid) + _nn(tri, lo3)
    acum_t = acum.T
    a_last = acum[l - 1:l, :]
    chunk_decay = jnp.exp(a_last)

    eh = eh_ref[...]
    xdt = xs * _expand2(dt, eh)
    xw = (xdt * _expand2(jnp.exp(a_last - acum), eh)).astype(BF16)
    xdtb = xdt.astype(BF16)
    upper = lax.broadcasted_iota(jnp.int32, (l, LANES), 1) < SSM_HEADDIM
    top_rows = lax.broadcasted_iota(jnp.int32, (2 * SSM_HEADDIM, SSM_STATE), 0) < SSM_HEADDIM
    zero = jnp.zeros((l, LANES), BF16)

    for g in range(SSM_GROUPS):
        b0 = SSM_WIDTH + SSM_STATE * g
        c0 = SSM_WIDTH + SSM_GROUPS * SSM_STATE + SSM_STATE * g
        bb = u[:, b0:b0 + SSM_STATE].astype(BF16)
        cc = u[:, c0:c0 + SSM_STATE].astype(BF16)
        cb = _nt(cc, bb)
        for r in range(0, HEADS_PER_GROUP, 2):
            h = g * HEADS_PER_GROUP + r
            p0 = SSM_HEADDIM * h
            ms = []
            for hh in (h, h + 1):
                seg = acum[:, hh:hh + 1] - acum_t[hh:hh + 1, :]
                ms.append((cb * jnp.exp(jnp.where(causal, seg, NEG))).astype(BF16))
            x2 = xdtb[:, p0:p0 + LANES]
            x_blockdiag = jnp.concatenate([jnp.where(upper, x2, zero), jnp.where(upper, zero, x2)], axis=0)
            yd_scr[:, p0:p0 + LANES] = _nn(jnp.concatenate(ms, axis=1), x_blockdiag)
            hs = h_scr[h:h + 2].reshape(2 * SSM_HEADDIM, SSM_STATE)
            yo_scr[:, p0:p0 + LANES] = _nt(cc, hs.astype(BF16))
            st = _tn(xw[:, p0:p0 + LANES], bb)
            decay = jnp.where(top_rows, chunk_decay[:, h:h + 1], chunk_decay[:, h + 1:h + 2])
            h_scr[h:h + 2] = (hs * decay + st).reshape(2, SSM_HEADDIM, SSM_STATE)

    y = yd_scr[...] + yo_scr[...] * _expand2(jnp.exp(acum), eh) + dsk_ref[...] * xs
    y_ref[...] = _rmsnorm(y * _silu(z_ref[...]), nw_ref[...])

    @pl.when(c == pl.num_programs(1) - 1)
    def _():
        hfin_ref[0] = h_scr[...]
        cfin_ref[0] = xp_scr[8 + l - hist:8 + l, :]


def _ssd_prompt(xbc, z, dt, conv_w, conv_b, dtb, alog, dsk, nw, eh, batch, seq):
    nc = seq // SSD_CHUNK
    row = lambda b, c: (b * nc + c, 0)
    const = lambda b, c: (0, 0)
    return pl.pallas_call(
        _ssd_prompt_kernel,
        grid=(batch, nc),
        in_specs=[
            pl.BlockSpec((SSD_CHUNK, XBC_WIDTH), row),
            pl.BlockSpec((SSD_CHUNK, SSM_WIDTH), row),
            pl.BlockSpec((SSD_CHUNK, DT_PAD), row),
            pl.BlockSpec((CONV_WIDTH, XBC_WIDTH), const),
            pl.BlockSpec((1, XBC_WIDTH), const),
            pl.BlockSpec((1, DT_PAD), const),
            pl.BlockSpec((1, DT_PAD), const),
            pl.BlockSpec((1, SSM_WIDTH), const),
            pl.BlockSpec((1, SSM_WIDTH), const),
            pl.BlockSpec((DT_PAD, SSM_WIDTH), const),
        ],
        out_specs=[
            pl.BlockSpec((SSD_CHUNK, SSM_WIDTH), row),
            pl.BlockSpec((1, SSM_HEADS, SSM_HEADDIM, SSM_STATE), lambda b, c: (b, 0, 0, 0)),
            pl.BlockSpec((1, CONV_WIDTH - 1, XBC_WIDTH), lambda b, c: (b, 0, 0)),
        ],
        out_shape=[
            jax.ShapeDtypeStruct((batch * seq, SSM_WIDTH), F32),
            jax.ShapeDtypeStruct((batch, SSM_HEADS, SSM_HEADDIM, SSM_STATE), F32),
            jax.ShapeDtypeStruct((batch, CONV_WIDTH - 1, XBC_WIDTH), F32),
        ],
        scratch_shapes=[
            pltpu.VMEM((SSM_HEADS, SSM_HEADDIM, SSM_STATE), F32),
            pltpu.VMEM((8 + SSD_CHUNK + 8, XBC_WIDTH), F32),
            pltpu.VMEM((SSD_CHUNK, SSM_WIDTH), F32),
            pltpu.VMEM((SSD_CHUNK, SSM_WIDTH), F32),
        ],
        compiler_params=pltpu.CompilerParams(
            dimension_semantics=("parallel", "arbitrary"), vmem_limit_bytes=VMEM_LIMIT),
        name="ssd_prompt",
    )(xbc, z, dt, conv_w, conv_b, dtb, alog, dsk, nw, eh)


def _attn_prompt_kernel(q_ref, g_ref, k_ref, v_ref, o_ref, kb_scr, vt_scr, km_scr, st_scr, *, k_sel):
    n = pl.program_id(2)
    nblk = kb_scr.shape[0]

    @pl.when(n == 0)
    def _():
        for j in range(nblk):
            kf = k_ref[MOBA_BLOCK * j:MOBA_BLOCK * (j + 1), :]
            kb_scr[j] = kf.astype(BF16)
            km_scr[j:j + 1, :] = jnp.mean(kf, axis=0, keepdims=True)
            vt = v_ref[MOBA_BLOCK * j:MOBA_BLOCK * (j + 1), :].T.astype(BF16)
            vt_scr[j] = jnp.concatenate([vt, jnp.ones((SUM_ROWS, MOBA_BLOCK), BF16)], axis=0)

    own = (n * Q_TILE) // MOBA_BLOCK
    qf = q_ref[...]
    qt = jnp.concatenate([qf[:, LANES * r:LANES * (r + 1)].T for r in range(Q_PER_KV)], axis=1)
    qtb = (qt * SOFTMAX_C).astype(BF16)
    n_cols = Q_PER_KV * Q_TILE

    sels = []
    if k_sel > 0:
        kh, kmid, _ = _split3(km_scr[...])
        qh, qmid, _ = _split3(qt)
        gate = _nn(kh, qh) + _nn(kh, qmid) + _nn(kmid, qh)
        rowf = lax.broadcasted_iota(jnp.int32, gate.shape, 0).astype(F32)
        gate = jnp.where(rowf < own.astype(F32), gate, NEG)
        for t in range(k_sel):
            m = jnp.max(gate, axis=0, keepdims=True)
            idx = jnp.min(jnp.where(gate == m, rowf, float(nblk)), axis=0, keepdims=True)
            sels.append(jnp.where(t < own, idx, -1.0))
            gate = jnp.where(rowf == idx, -jnp.inf, gate)

    def scores(j):
        return _nn(kb_scr[j], qtb)

    def chose(j):
        jf = lax.convert_element_type(j, F32)
        hit = sels[0] == jf
        for t in range(1, k_sel):
            hit = jnp.logical_or(hit, sels[t] == jf)
        return hit

    kpos = own * MOBA_BLOCK + lax.broadcasted_iota(jnp.int32, (MOBA_BLOCK, n_cols), 0)
    qpos = n * Q_TILE + (lax.broadcasted_iota(jnp.int32, (MOBA_BLOCK, n_cols), 1) & (Q_TILE - 1))
    st = jnp.where(kpos <= qpos, scores(own), NEG)
    m = jnp.max(st, axis=0, keepdims=True)
    acc = _nn(vt_scr[own], jnp.exp2(st - m).astype(BF16))

    n_groups = (own + ATTN_GROUP - 1) // ATTN_GROUP

    def produce(grp):
        mg = jnp.full((1, n_cols), NEG, F32)
        for u in range(ATTN_GROUP):
            j = grp * ATTN_GROUP + u
            st = scores(jnp.minimum(j, nblk - 1))
            st_scr[grp % 2, u] = st
            mg = jnp.maximum(mg, jnp.where(chose(j), jnp.max(st, axis=0, keepdims=True), NEG))
        return mg

    if k_sel > 0:
        def body(grp, carry):
            m, acc, mg = carry
            m_new = jnp.maximum(m, mg)
            alpha = jnp.exp2(m - m_new)
            acc = alpha * acc
            for u in range(ATTN_GROUP):
                j = grp * ATTN_GROUP + u
                p = jnp.exp2(st_scr[grp % 2, u] - jnp.where(chose(j), m_new, -NEG))
                acc = acc + _nn(vt_scr[jnp.minimum(j, nblk - 1)], p.astype(BF16))
            mg_next = produce(grp + 1)
            return m_new, acc, mg_next
        _, acc, _ = lax.fori_loop(0, n_groups, body, (m, acc, produce(0)))

    o = acc[:ATTN_HEADDIM] / acc[ATTN_HEADDIM:ATTN_HEADDIM + 1]
    gg = _silu(g_ref[...])
    for r in range(Q_PER_KV):
        o_ref[:, LANES * r:LANES * (r + 1)] = (
            o[:, Q_TILE * r:Q_TILE * (r + 1)].T * gg[:, LANES * r:LANES * (r + 1)])


def _attn_prompt(q, g, k, v, batch, seq):
    nq = seq // Q_TILE
    nblk = seq // MOBA_BLOCK
    k_sel = min(MOBA_TOPK, nblk - 1)
    qmap = lambda b, h, n: (b * nq + n, h)
    kvmap = lambda b, h, n: (b, h)
    width = Q_PER_KV * ATTN_HEADDIM
    return pl.pallas_call(
        functools.partial(_attn_prompt_kernel, k_sel=k_sel),
        grid=(batch, KV_HEADS, nq),
        in_specs=[
            pl.BlockSpec((Q_TILE, width), qmap),
            pl.BlockSpec((Q_TILE, width), qmap),
            pl.BlockSpec((seq, ATTN_HEADDIM), kvmap),
            pl.BlockSpec((seq, ATTN_HEADDIM), kvmap),
        ],
        out_specs=pl.BlockSpec((Q_TILE, width), qmap),
        out_shape=jax.ShapeDtypeStruct((batch * seq, ATTN_WIDTH), F32),
        scratch_shapes=[
            pltpu.VMEM((nblk, MOBA_BLOCK, ATTN_HEADDIM), BF16),
            pltpu.VMEM((nblk, ATTN_HEADDIM + SUM_ROWS, MOBA_BLOCK), BF16),
            pltpu.VMEM((nblk, ATTN_HEADDIM), F32),
            pltpu.VMEM((2, ATTN_GROUP, MOBA_BLOCK, Q_PER_KV * Q_TILE), F32),
        ],
        compiler_params=pltpu.CompilerParams(
            dimension_semantics=("parallel", "parallel", "arbitrary"),
            vmem_limit_bytes=VMEM_LIMIT),
        name="moba_prompt",
    )(q, g, k, v)


def _out_kernel(ys_ref, ya_ref, x_ref, w1_ref, w2_ref, nw_ref, o_ref):
    out = x_ref[...] + _nn(ys_ref[...].astype(BF16), w1_ref[...])
    out = out + _nn(ya_ref[...].astype(BF16), w2_ref[...])
    o_ref[...] = _rmsnorm(out, nw_ref[...])


def _out_proj(ys, ya, x, w1, w2, nw, tm):
    t = x.shape[0]
    row = lambda i: (i, 0)
    const = lambda i: (0, 0)
    return pl.pallas_call(
        _out_kernel,
        grid=(t // tm,),
        in_specs=[
            pl.BlockSpec((tm, SSM_WIDTH), row),
            pl.BlockSpec((tm, ATTN_WIDTH), row),
            pl.BlockSpec((tm, D_MODEL), row),
            pl.BlockSpec((SSM_WIDTH, D_MODEL), const),
            pl.BlockSpec((ATTN_WIDTH, D_MODEL), const),
            pl.BlockSpec((1, D_MODEL), const),
        ],
        out_specs=pl.BlockSpec((tm, D_MODEL), row),
        out_shape=jax.ShapeDtypeStruct((t, D_MODEL), F32),
        compiler_params=pltpu.CompilerParams(
            dimension_semantics=("parallel",), vmem_limit_bytes=VMEM_LIMIT),
        name="out_proj",
    )(ys, ya, x, w1, w2, nw)


def _ssd_sample_kernel(xbc_ref, cst_ref, z_ref, dt_ref, h0_ref, cw_ref, cb_ref, dtb_ref, alog_ref,
                       dsk_ref, nw_ref, eh_ref, eg_ref,
                       y_ref, hnew_ref, cnew_ref, *, steps):
    tb = SAMPLE_TILE
    hist = CONV_WIDTH - 1
    rows = [cst_ref[:, j, :] for j in range(hist)] + [xbc_ref[:, t, :] for t in range(steps)]
    for j in range(hist):
        cnew_ref[:, j, :] = rows[steps + j]

    a_neg = -jnp.exp(alog_ref[...])
    u, dts, acs = [], [], []
    run = None
    for t in range(steps):
        conv = cb_ref[...]
        for i in range(CONV_WIDTH):
            conv = conv + rows[t + i] * cw_ref[i:i + 1, :]
        u.append(_silu(conv))
        dt = _softplus(dt_ref[:, t, :] + dtb_ref[...])
        dts.append(dt)
        run = dt * a_neg if run is None else run + dt * a_neg
        acs.append(run)
    uu = jnp.concatenate(u, axis=0)
    xs = uu[:, :SSM_WIDTH]
    bm = uu[:, SSM_WIDTH:SSM_WIDTH + SSM_GROUPS * SSM_STATE]
    cm = uu[:, SSM_WIDTH + SSM_GROUPS * SSM_STATE:]
    dt_all = jnp.concatenate(dts, axis=0)
    ac_all = jnp.concatenate(acs, axis=0)
    a_last = jnp.concatenate([acs[-1]] * steps, axis=0)

    eh = eh_ref[...]
    xdt = xs * _nn_exact_rhs(dt_all, eh)
    xw = xs * _nn_exact_rhs(dt_all * jnp.exp(a_last - ac_all), eh)
    ea = _nn_exact_rhs(jnp.exp(ac_all), eh)

    eg = eg_ref[...]
    ydiag = []
    for t in range(steps):
        acc = None
        for j in range(t + 1):
            cbx = _nn_exact_rhs(cm[tb * t:tb * (t + 1)] * bm[tb * j:tb * (j + 1)], eg)
            lmx = _nn_exact_rhs(jnp.exp(acs[t] - acs[j]), eh)
            term = cbx * lmx * xdt[tb * j:tb * (j + 1)]
            acc = term if acc is None else acc + term
        ydiag.append(acc)
    y = jnp.concatenate(ydiag, axis=0) + dsk_ref[...] * xs

    rowb = lax.broadcasted_iota(jnp.int32, (steps * tb, 1), 0) % tb
    cmb = cm.astype(BF16)
    gw = HEADS_PER_GROUP * SSM_HEADDIM
    yoff = [None] * SSM_GROUPS
    for b in range(tb):
        mine = rowb == b
        for g in range(SSM_GROUPS):
            h0g = h0_ref[b, HEADS_PER_GROUP * g:HEADS_PER_GROUP * (g + 1)].reshape(gw, SSM_STATE)
            part = jnp.where(mine, _nt(cmb[:, SSM_STATE * g:SSM_STATE * (g + 1)], h0g.astype(BF16)), 0.0)
            yoff[g] = part if yoff[g] is None else yoff[g] + part
            xwb = jnp.where(mine, xw[:, gw * g:gw * (g + 1)], 0.0)
            xh, xm, _ = _split3(xwb)
            bh, bmid, _ = _split3(bm[:, SSM_STATE * g:SSM_STATE * (g + 1)])
            st = _tn(xh, bh) + _tn(xh, bmid) + _tn(xm, bh)
            for r in range(HEADS_PER_GROUP):
                h = HEADS_PER_GROUP * g + r
                dec = jnp.exp(acs[-1][b:b + 1, h:h + 1])
                hnew_ref[b, h] = (h0g[SSM_HEADDIM * r:SSM_HEADDIM * (r + 1)] * dec
                                  + st[SSM_HEADDIM * r:SSM_HEADDIM * (r + 1)])
    y = y + jnp.concatenate(yoff, axis=1) * ea

    for t in range(steps):
        yt = y[tb * t:tb * (t + 1)] * _silu(z_ref[:, t, :])
        y_ref[:, t, :] = _rmsnorm(yt, nw_ref[...])


def _ssd_sample(xbc, cst, z, dt, h0, conv_w, conv_b, dtb, alog, dsk, nw, eh, eg):
    nb, steps = xbc.shape[0], xbc.shape[1]
    tb = SAMPLE_TILE
    c2 = lambda i: (0, 0)
    b3 = lambda i: (i, 0, 0)
    b4 = lambda i: (i, 0, 0, 0)
    hist = CONV_WIDTH - 1
    return pl.pallas_call(
        functools.partial(_ssd_sample_kernel, steps=steps),
        grid=(nb // tb,),
        in_specs=[
            pl.BlockSpec((tb, steps, XBC_WIDTH), b3),
            pl.BlockSpec((tb, hist, XBC_WIDTH), b3),
            pl.BlockSpec((tb, steps, SSM_WIDTH), b3),
            pl.BlockSpec((tb, steps, DT_PAD), b3),
            pl.BlockSpec((tb, SSM_HEADS, SSM_HEADDIM, SSM_STATE), b4),
            pl.BlockSpec((CONV_WIDTH, XBC_WIDTH), c2),
            pl.BlockSpec((1, XBC_WIDTH), c2),
            pl.BlockSpec((1, DT_PAD), c2),
            pl.BlockSpec((1, DT_PAD), c2),
            pl.BlockSpec((1, SSM_WIDTH), c2),
            pl.BlockSpec((1, SSM_WIDTH), c2),
            pl.BlockSpec((DT_PAD, SSM_WIDTH), c2),
            pl.BlockSpec((SSM_GROUPS * SSM_STATE, SSM_WIDTH), c2),
        ],
        out_specs=[
            pl.BlockSpec((tb, steps, SSM_WIDTH), b3),
            pl.BlockSpec((tb, SSM_HEADS, SSM_HEADDIM, SSM_STATE), b4),
            pl.BlockSpec((tb, hist, XBC_WIDTH), b3),
        ],
        out_shape=[
            jax.ShapeDtypeStruct((nb, steps, SSM_WIDTH), F32),
            jax.ShapeDtypeStruct((nb, SSM_HEADS, SSM_HEADDIM, SSM_STATE), F32),
            jax.ShapeDtypeStruct((nb, hist, XBC_WIDTH), F32),
        ],
        compiler_params=pltpu.CompilerParams(
            dimension_semantics=("parallel",), vmem_limit_bytes=VMEM_LIMIT),
        name="ssd_sample",
    )(xbc, cst, z, dt, h0, conv_w, conv_b, dtb, alog, dsk, nw, eh, eg)


def _attn_sample_kernel(pt_ref, q_ref, g_ref, kn_ref, vn_ref, ck_ref, cv_ref, o_ref,
                        kbuf, vbuf, km_scr, w_scr, idx_v, idx_s, o_scr, sems, *, n_pages, steps):
    b = pl.program_id(0)
    slot = b % 2
    n_full = n_pages // PAGES_PER_BLOCK
    page_rows = PAGE_SIZE * KV_HEADS
    n_cols = ATTN_HEADS * steps
    per_kv = Q_PER_KV * steps

    def k_copy(seq, sl, p):
        return pltpu.make_async_copy(ck_ref.at[pt_ref[seq * n_pages + p]], kbuf.at[sl, p], sems.at[sl])

    def v_copy(seq, p):
        return pltpu.make_async_copy(cv_ref.at[pt_ref[seq * n_pages + p]], vbuf.at[p], sems.at[2])

    @pl.when(b == 0)
    def _():
        for p in range(n_pages):
            k_copy(0, 0, p).start(priority=p % 2)
        for p in range(n_pages):
            v_copy(0, p).start(priority=p % 2)

    for p in range(n_pages):
        k_copy(b, slot, p).wait()

    km_scr[...] = jnp.zeros_like(km_scr)
    for blk in range(n_full):
        tot = None
        for pp in range(PAGES_PER_BLOCK):
            page = kbuf[slot, PAGES_PER_BLOCK * blk + pp]
            part = page.reshape(page_rows // 8, 8, LANES).sum(axis=0)
            tot = part if tot is None else tot + part
        km_scr[KV_HEADS * blk:KV_HEADS * (blk + 1), :] = (
            (tot[:KV_HEADS] + tot[KV_HEADS:]) * (1.0 / MOBA_BLOCK))

    qf = q_ref[0]
    q_all = jnp.concatenate([qf[:, LANES * h:LANES * (h + 1)] for h in range(ATTN_HEADS)], axis=0)
    q_pad = jnp.concatenate([q_all, jnp.zeros((LANES - n_cols, LANES), F32)], axis=0)

    gate = _nt_x3(q_pad, km_scr[...])
    lane = lax.broadcasted_iota(jnp.int32, gate.shape, 1)
    rowc = lax.broadcasted_iota(jnp.int32, gate.shape, 0)
    ok = jnp.logical_and(lane % KV_HEADS == rowc // per_kv, lane < KV_HEADS * n_full)
    gate = jnp.where(ok, gate, -jnp.inf)
    lanef = lane.astype(F32)
    chosen = jnp.zeros(gate.shape, jnp.int32)
    for t in range(MOBA_TOPK):
        m = jnp.max(gate, axis=-1, keepdims=True)
        idx = jnp.min(jnp.where(gate == m, lanef, float(4 * LANES)), axis=-1, keepdims=True)
        chosen = jnp.where(lane == t, idx.astype(jnp.int32) // KV_HEADS, chosen)
        gate = jnp.where(lanef == idx, -jnp.inf, gate)
    idx_v[...] = chosen
    to_smem = pltpu.make_async_copy(idx_v, idx_s, sems.at[3])
    to_smem.start()

    q_t = q_pad.T
    for c in range(n_cols):
        w_scr[c] = jnp.broadcast_to(q_t[:, c:c + 1], (ATTN_HEADDIM, LANES)).astype(BF16)

    to_smem.wait()
    for p in range(n_pages):
        v_copy(b, p).wait()

    @pl.when(b + 1 < pl.num_programs(0))
    def _():
        for p in range(n_pages):
            k_copy(b + 1, 1 - slot, p).start(priority=p % 2)

    sub8 = lax.broadcasted_iota(jnp.int32, (8, LANES), 0)

    def k_tile(blk, g):
        return jnp.concatenate(
            [kbuf[slot, PAGES_PER_BLOCK * blk + pp, pl.ds(g, PAGE_SIZE, stride=KV_HEADS), :]
             for pp in range(PAGES_PER_BLOCK)], axis=0)

    def v_tile(blk, g):
        return jnp.concatenate(
            [vbuf[PAGES_PER_BLOCK * blk + pp, pl.ds(g, PAGE_SIZE, stride=KV_HEADS), :]
             for pp in range(PAGES_PER_BLOCK)], axis=0)

    def column(c, i, g, knb, vn):
        w = w_scr[c]
        s_own = jnp.where(sub8 <= i % steps, _nn(knb, w) * SOFTMAX_C, NEG)
        mx = jnp.max(s_own, axis=0, keepdims=True)
        blks = [idx_s[c, t] for t in range(MOBA_TOPK)]
        scores = []
        for blk in blks:
            s = _nn(k_tile(blk, g).astype(BF16), w) * SOFTMAX_C
            scores.append(s)
            mx = jnp.maximum(mx, jnp.max(s, axis=0, keepdims=True))
        p = jnp.exp2(s_own - mx)
        l = jnp.sum(p, axis=0, keepdims=True)
        acc = jnp.sum(p * vn, axis=0, keepdims=True)
        for blk, s in zip(blks, scores):
            p = jnp.exp2(s - mx)
            l = l + jnp.sum(p, axis=0, keepdims=True)
            acc = acc + jnp.sum(p * v_tile(blk, g), axis=0, keepdims=True)
        return acc / l

    for g in range(KV_HEADS):
        pad = jnp.zeros((8 - steps, LANES), F32)
        kn = jnp.concatenate([kn_ref[0][:, LANES * g:LANES * (g + 1)], pad], axis=0)
        vn = jnp.concatenate([vn_ref[0][:, LANES * g:LANES * (g + 1)], pad], axis=0)
        knb = kn.astype(BF16)

        def pair_body(ip, _, g=g, knb=knb, vn=vn):
            for u in range(2):
                i = 2 * ip + u
                c = g * per_kv + i
                o_scr[pl.ds(c, 1), :] = column(c, i, g, knb, vn)
            return 0

        lax.fori_loop(0, per_kv // 2, pair_body, 0)

    @pl.when(b + 1 < pl.num_programs(0))
    def _():
        for p in range(n_pages):
            v_copy(b + 1, p).start(priority=p % 2)

    gg = _silu(g_ref[0])
    for h in range(ATTN_HEADS):
        o_ref[0, :, LANES * h:LANES * (h + 1)] = (
            o_scr[steps * h:steps * (h + 1), :] * gg[:, LANES * h:LANES * (h + 1)])


def _attn_sample(page_table, q, g, k_new, v_new, cache_k, cache_v):
    nb, steps = q.shape[0], q.shape[1]
    n_pages = page_table.shape[1]
    n_pool = cache_k.shape[0]
    page_rows = PAGE_SIZE * KV_HEADS
    ck = cache_k.reshape(n_pool, page_rows, ATTN_HEADDIM)
    cv = cache_v.reshape(n_pool, page_rows, ATTN_HEADDIM)
    b3 = lambda i, pt: (i, 0, 0)
    grid_spec = pltpu.PrefetchScalarGridSpec(
        num_scalar_prefetch=1,
        grid=(nb,),
        in_specs=[
            pl.BlockSpec((1, steps, ATTN_WIDTH), b3),
            pl.BlockSpec((1, steps, ATTN_WIDTH), b3),
            pl.BlockSpec((1, steps, KV_WIDTH), b3),
            pl.BlockSpec((1, steps, KV_WIDTH), b3),
            pl.BlockSpec(memory_space=pl.ANY),
            pl.BlockSpec(memory_space=pl.ANY),
        ],
        out_specs=pl.BlockSpec((1, steps, ATTN_WIDTH), b3),
        scratch_shapes=[
            pltpu.VMEM((2, n_pages, page_rows, ATTN_HEADDIM), F32),
            pltpu.VMEM((n_pages, page_rows, ATTN_HEADDIM), F32),
            pltpu.VMEM((LANES, ATTN_HEADDIM), F32),
            pltpu.VMEM((ATTN_HEADS * steps, ATTN_HEADDIM, LANES), BF16),
            pltpu.VMEM((LANES, LANES), jnp.int32),
            pltpu.SMEM((LANES, LANES), jnp.int32),
            pltpu.VMEM((ATTN_HEADS * steps, ATTN_HEADDIM), F32),
            pltpu.SemaphoreType.DMA((4,)),
        ],
    )
    return pl.pallas_call(
        functools.partial(_attn_sample_kernel, n_pages=n_pages, steps=steps),
        grid_spec=grid_spec,
        out_shape=jax.ShapeDtypeStruct((nb, steps, ATTN_WIDTH), F32),
        compiler_params=pltpu.CompilerParams(
            dimension_semantics=("arbitrary",), vmem_limit_bytes=VMEM_LIMIT),
        name="moba_sample",
    )(page_table.reshape(-1), q, g, k_new, v_new, ck, cv)


def _rope_tables(pos):
    inv = 1.0 / (ROPE_THETA ** (jnp.arange(0, ROT_DIM, 2, dtype=F32) / ROT_DIM))
    ang = pos.astype(F32)[:, None] * inv[None, :]
    cos, sin = jnp.cos(ang), jnp.sin(ang)
    n = pos.shape[0]
    rest = LANES - ROT_DIM
    cosf = jnp.concatenate([cos, cos, jnp.ones((n, rest), F32)], axis=1)
    sinf = jnp.concatenate([-sin, sin, jnp.zeros((n, rest), F32)], axis=1)
    return cosf, sinf


def _row_tile(t, cap):
    tm = cap
    while t % tm:
        tm //= 2
    return tm


def kernel(x_prompt, x_sample, cache_k, cache_v, state_ssm, state_conv, page_table,
           norm_w, w_in, conv_w, conv_b, dt_bias, a_log, d_skip, ssm_norm_w, w_out, final_norm_w):
    depth = w_in.shape[0]
    assert depth == 1
    bp, seq, _ = x_prompt.shape
    nb, steps, _ = x_sample.shape
    n_pages = page_table.shape[1]
    past = n_pages * PAGE_SIZE
    assert seq % MOBA_BLOCK == 0 and past % MOBA_BLOCK == 0 and steps <= 8
    assert n_pages // PAGES_PER_BLOCK >= MOBA_TOPK and nb % SAMPLE_TILE == 0
    assert KV_HEADS * (n_pages // PAGES_PER_BLOCK) <= LANES

    w = w_in[0].astype(BF16)
    c_dt = SSM_WIDTH + XBC_WIDTH
    w_cat = jnp.concatenate(
        [w[:, :c_dt], w[:, c_dt + SSM_HEADS:], w[:, c_dt:c_dt + SSM_HEADS],
         jnp.zeros((D_MODEL, DT_PAD - SSM_HEADS), BF16)], axis=1)
    nw = norm_w[0].reshape(1, D_MODEL)
    pad_h = lambda v: jnp.concatenate([v, jnp.zeros((DT_PAD - SSM_HEADS,), F32)]).reshape(1, DT_PAD)
    dtb, alog = pad_h(dt_bias[0]), pad_h(a_log[0])
    dsk = jnp.repeat(d_skip[0], SSM_HEADDIM).reshape(1, SSM_WIDTH)
    snw = ssm_norm_w[0].reshape(1, SSM_WIDTH)
    cb = conv_b[0].reshape(1, XBC_WIDTH)
    w1 = w_out[0][:SSM_WIDTH].astype(BF16)
    w2 = w_out[0][SSM_WIDTH:].astype(BF16)
    fnw = final_norm_w.reshape(1, D_MODEL)

    heads = jnp.arange(DT_PAD)[:, None]
    lanes = jnp.arange(SSM_WIDTH)[None, :]
    eh = (lanes // SSM_HEADDIM == heads).astype(BF16)
    rows_g = jnp.arange(SSM_GROUPS * SSM_STATE)[:, None] // SSM_STATE
    eg = (lanes // (HEADS_PER_GROUP * SSM_HEADDIM) == rows_g).astype(BF16)

    xp = x_prompt.reshape(bp * seq, D_MODEL)
    tm_p = _row_tile(seq, 256)
    cos_p, sin_p = _rope_tables(jnp.arange(seq))
    z, xbc, q, k, v, g, dt, k_rows, v_rows = _proj(xp, nw, w_cat, cos_p, sin_p, tm_p)
    y_ssm, h_p, c_p = _ssd_prompt(xbc, z, dt, conv_w[0], cb, dtb, alog, dsk, snw, eh, bp, seq)
    y_attn = _attn_prompt(q, g, k, v, bp, seq)
    y_p = _out_proj(y_ssm, y_attn, xp, w1, w2, fnw, _row_tile(bp * seq, 512))

    ts = nb * steps
    xs = x_sample.reshape(ts, D_MODEL)
    tm_s = _row_tile(ts, 256)
    cos_s, sin_s = _rope_tables(past + jnp.arange(steps))
    cos_s, sin_s = jnp.tile(cos_s, (tm_s // steps, 1)), jnp.tile(sin_s, (tm_s // steps, 1))
    zs, xbcs, qs, ks, vs, gs, dts, ks_rows, vs_rows = _proj(xs, nw, w_cat, cos_s, sin_s, tm_s)
    r3 = lambda a: a.reshape(nb, steps, a.shape[-1])
    ys_ssm, h_s, c_s = _ssd_sample(r3(xbcs), state_conv[0], r3(zs), r3(dts), state_ssm[0],
                                   conv_w[0], cb, dtb, alog, dsk, snw, eh, eg)
    ys_attn = _attn_sample(page_table, r3(qs), r3(gs), r3(ks), r3(vs), cache_k[0], cache_v[0])
    y_s = _out_proj(ys_ssm.reshape(ts, SSM_WIDTH), ys_attn.reshape(ts, ATTN_WIDTH), xs, w1, w2, fnw,
                    _row_tile(ts, 512))

    kv5 = lambda a, n, s: a.reshape(1, n, s, KV_HEADS, ATTN_HEADDIM)
    return (y_p.reshape(bp, seq, D_MODEL), y_s.reshape(nb, steps, D_MODEL),
            kv5(k_rows, bp, seq), kv5(v_rows, bp, seq), kv5(ks_rows, nb, steps), kv5(vs_rows, nb, steps),
            h_p[None], c_p[None], h_s[None], c_s[None])
```

```python
import functools

import jax
import jax.numpy as jnp
from jax import lax
from jax.experimental import pallas as pl
from jax.experimental.pallas import tpu as pltpu

F32 = jnp.float32
BF16 = jnp.bfloat16

D_MODEL = 1024
SSM_WIDTH = 1024
SSM_HEADDIM = 64
SSM_HEADS = SSM_WIDTH // SSM_HEADDIM
SSM_GROUPS = 4
HEADS_PER_GROUP = SSM_HEADS // SSM_GROUPS
SSM_STATE = 128
CONV_WIDTH = 4
SSD_CHUNK = 128
XBC_WIDTH = SSM_WIDTH + 2 * SSM_GROUPS * SSM_STATE
ATTN_HEADDIM = 128
ATTN_HEADS = 8
KV_HEADS = 4
Q_PER_KV = ATTN_HEADS // KV_HEADS
ATTN_WIDTH = ATTN_HEADS * ATTN_HEADDIM
KV_WIDTH = KV_HEADS * ATTN_HEADDIM
MOBA_BLOCK = 256
MOBA_TOPK = 3
Q_TILE = 256
ROT_DIM = 32
ROPE_THETA = 500000.0
PAGE_SIZE = 128
PAGES_PER_BLOCK = MOBA_BLOCK // PAGE_SIZE
EPS = 1e-6
NEG = -1e30
SOFTMAX_C = ATTN_HEADDIM ** -0.5 * 1.4426950408889634
LANES = 128
DT_PAD = LANES
VMEM_LIMIT = 56 * 1024 * 1024
SAMPLE_TILE = 8
ATTN_GROUP = 4
SUM_ROWS = 16


def _nt(a, b):
    return lax.dot_general(a, b, (((1,), (1,)), ((), ())), preferred_element_type=F32)


def _tn(a, b):
    return lax.dot_general(a, b, (((0,), (0,)), ((), ())), preferred_element_type=F32)


def _nn(a, b):
    return jnp.dot(a, b, preferred_element_type=F32)


def _split3(x):
    hi = x.astype(BF16)
    r1 = x - hi.astype(F32)
    mid = r1.astype(BF16)
    lo = (r1 - mid.astype(F32)).astype(BF16)
    return hi, mid, lo


def _nn_exact_rhs(x, e):
    hi, mid, lo = _split3(x)
    return _nn(hi, e) + _nn(mid, e) + _nn(lo, e)


def _expand2(x, e):
    hi = x.astype(BF16)
    mid = (x - hi.astype(F32)).astype(BF16)
    return _nn(hi, e) + _nn(mid, e)


def _nt_x3(a, b):
    ah, am, _ = _split3(a)
    bh, bm, _ = _split3(b)
    return _nt(ah, bh) + _nt(ah, bm) + _nt(am, bh)


def _silu(x):
    return (0.5 * x) * (1.0 + jnp.tanh(0.5 * x))


def _softplus(x):
    return jnp.maximum(x, 0.0) + jnp.log1p(jnp.exp(-jnp.abs(x)))


def _rmsnorm(x, w):
    ms = jnp.mean(x * x, axis=-1, keepdims=True)
    return x * lax.rsqrt(ms + EPS) * w


def _proj_kernel(x_ref, nw_ref, wa_ref, wb_ref, wdt_ref, cos_ref, sin_ref,
                 z_ref, xbc_ref, q_ref, k_ref, v_ref, g_ref, dt_ref, krows_ref, vrows_ref):
    h = _rmsnorm(x_ref[...], nw_ref[...]).astype(BF16)
    tm = x_ref.shape[0]

    z_ref[...] = _nn(h, wa_ref[:, :SSM_WIDTH])
    xbc_ref[...] = _nn(h, wa_ref[:, SSM_WIDTH:])
    o = 0
    q = _nn(h, wb_ref[:, o:o + ATTN_WIDTH]); o += ATTN_WIDTH
    k = _nn(h, wb_ref[:, o:o + KV_WIDTH]); o += KV_WIDTH
    v = _nn(h, wb_ref[:, o:o + KV_WIDTH]); o += KV_WIDTH
    v_ref[...] = v
    for hh in range(KV_HEADS):
        vrows_ref[pl.ds(hh, tm, stride=KV_HEADS), :] = v[:, LANES * hh:LANES * (hh + 1)]
    g_ref[...] = _nn(h, wb_ref[:, o:o + ATTN_WIDTH])
    dt_ref[...] = _nn(h, wdt_ref[...])

    cosf = cos_ref[...]
    sinf = sin_ref[...]
    first_half = lax.broadcasted_iota(jnp.int32, cosf.shape, 1) < ROT_DIM // 2

    def rope(t, n_heads, out_ref, rows_ref=None):
        for hh in range(n_heads):
            th = t[:, LANES * hh:LANES * (hh + 1)]
            partner = jnp.where(first_half,
                                pltpu.roll(th, LANES - ROT_DIM // 2, 1),
                                pltpu.roll(th, ROT_DIM // 2, 1))
            roped = th * cosf + partner * sinf
            out_ref[:, LANES * hh:LANES * (hh + 1)] = roped
            if rows_ref is not None:
                rows_ref[pl.ds(hh, tm, stride=n_heads), :] = roped

    rope(q, ATTN_HEADS, q_ref)
    rope(k, KV_HEADS, k_ref, krows_ref)


def _proj(x, norm_w, wa, wb, wdt, cosf, sinf, tm):
    t = x.shape[0]
    n_tab = cosf.shape[0] // tm
    row = lambda i: (i, 0)
    const = lambda i: (0, 0)
    widths = (SSM_WIDTH, XBC_WIDTH, ATTN_WIDTH, KV_WIDTH, KV_WIDTH, ATTN_WIDTH, DT_PAD)
    return pl.pallas_call(
        _proj_kernel,
        grid=(t // tm,),
        in_specs=[
            pl.BlockSpec((tm, D_MODEL), row),
            pl.BlockSpec((1, D_MODEL), const),
            pl.BlockSpec(wa.shape, const, pipeline_mode=pl.Buffered(1)),
            pl.BlockSpec(wb.shape, const, pipeline_mode=pl.Buffered(1)),
            pl.BlockSpec(wdt.shape, const, pipeline_mode=pl.Buffered(1)),
            pl.BlockSpec((tm, LANES), lambda i: (i % n_tab, 0)),
            pl.BlockSpec((tm, LANES), lambda i: (i % n_tab, 0)),
        ],
        out_specs=([pl.BlockSpec((tm, w), row) for w in widths]
                   + [pl.BlockSpec((tm * KV_HEADS, ATTN_HEADDIM), row)] * 2),
        out_shape=([jax.ShapeDtypeStruct((t, w), F32) for w in widths]
                   + [jax.ShapeDtypeStruct((t * KV_HEADS, ATTN_HEADDIM), F32)] * 2),
        compiler_params=pltpu.CompilerParams(
            dimension_semantics=("parallel",), vmem_limit_bytes=VMEM_LIMIT),
        name="in_proj",
    )(x, norm_w, wa, wb, wdt, cosf, sinf)


def _ssd_prompt_kernel(xbc_ref, z_ref, dt_ref, cw_ref, cb_ref, dtb_ref, alog_ref, dsk_ref, nw_ref, eh_ref,
                       y_ref, hfin_ref, cfin_ref, h_scr, xp_scr, yd_scr, yo_scr):
    c = pl.program_id(1)
    l = SSD_CHUNK
    hist = CONV_WIDTH - 1

    @pl.when(c == 0)
    def _():
        h_scr[...] = jnp.zeros_like(h_scr)
        xp_scr[0:8, :] = jnp.zeros((8, XBC_WIDTH), F32)

    xp_scr[8:8 + l, :] = xbc_ref[...]
    conv = cb_ref[...]
    for i in range(CONV_WIDTH):
        conv = conv + xp_scr[8 - hist + i:8 - hist + i + l, :] * cw_ref[i:i + 1, :]
    xp_scr[0:8, :] = xp_scr[l:l + 8, :]
    u = _silu(conv)
    xs = u[:, :SSM_WIDTH]

    dt = _softplus(dt_ref[...] + dtb_ref[...])
    a_neg = -jnp.exp(alog_ref[...])
    row = lax.broadcasted_iota(jnp.int32, (l, l), 0)
    col = lax.broadcasted_iota(jnp.int32, (l, l), 1)
    causal = col <= row
    tri = jnp.where(causal, 1.0, 0.0).astype(BF16)
    hi, mid, lo3 = _split3(dt * a_neg)
    acum = _nn(tri, hi) + _nn(tri, mid) + _nn(tri, lo3)
    acum_t = acum.T
    a_last = acum[l - 1:l, :]
    chunk_decay = jnp.exp(a_last)

    eh = eh_ref[...]
    xdt = xs * _expand2(dt, eh)
    xw = (xdt * _expand2(jnp.exp(a_last - acum), eh)).astype(BF16)
    xdtb = xdt.astype(BF16)
    upper = lax.broadcasted_iota(jnp.int32, (l, LANES), 1) < SSM_HEADDIM
    top_rows = lax.broadcasted_iota(jnp.int32, (2 * SSM_HEADDIM, SSM_STATE), 0) < SSM_HEADDIM
    zero = jnp.zeros((l, LANES), BF16)

    for g in range(SSM_GROUPS):
        b0 = SSM_WIDTH + SSM_STATE * g
        c0 = SSM_WIDTH + SSM_GROUPS * SSM_STATE + SSM_STATE * g
        bb = u[:, b0:b0 + SSM_STATE].astype(BF16)
        cc = u[:, c0:c0 + SSM_STATE].astype(BF16)
        cb = _nt(cc, bb)
        for r in range(0, HEADS_PER_GROUP, 2):
            h = g * HEADS_PER_GROUP + r
            p0 = SSM_HEADDIM * h
            ms = []
            for hh in (h, h + 1):
                seg = acum[:, hh:hh + 1] - acum_t[hh:hh + 1, :]
                ms.append((cb * jnp.exp(jnp.where(causal, seg, NEG))).astype(BF16))
            x2 = xdtb[:, p0:p0 + LANES]
            x_blockdiag = jnp.concatenate([jnp.where(upper, x2, zero), jnp.where(upper, zero, x2)], axis=0)
            yd_scr[:, p0:p0 + LANES] = _nn(jnp.concatenate(ms, axis=1), x_blockdiag)
            hs = h_scr[h:h + 2].reshape(2 * SSM_HEADDIM, SSM_STATE)
            yo_scr[:, p0:p0 + LANES] = _nt(cc, hs.astype(BF16))
            st = _tn(xw[:, p0:p0 + LANES], bb)
            decay = jnp.where(top_rows, chunk_decay[:, h:h + 1], chunk_decay[:, h + 1:h + 2])
            h_scr[h:h + 2] = (hs * decay + st).reshape(2, SSM_HEADDIM, SSM_STATE)

    y = yd_scr[...] + yo_scr[...] * _expand2(jnp.exp(acum), eh) + dsk_ref[...] * xs
    y_ref[...] = _rmsnorm(y * _silu(z_ref[...]), nw_ref[...]).astype(y_ref.dtype)

    @pl.when(c == pl.num_programs(1) - 1)
    def _():
        hfin_ref[0] = h_scr[...]
        cfin_ref[0] = xp_scr[8 + l - hist:8 + l, :]


def _ssd_prompt(xbc, z, dt, conv_w, conv_b, dtb, alog, dsk, nw, eh, batch, seq):
    nc = seq // SSD_CHUNK
    row = lambda b, c: (b * nc + c, 0)
    const = lambda b, c: (0, 0)
    return pl.pallas_call(
        _ssd_prompt_kernel,
        grid=(batch, nc),
        in_specs=[
            pl.BlockSpec((SSD_CHUNK, XBC_WIDTH), row),
            pl.BlockSpec((SSD_CHUNK, SSM_WIDTH), row),
            pl.BlockSpec((SSD_CHUNK, DT_PAD), row),
            pl.BlockSpec((CONV_WIDTH, XBC_WIDTH), const),
            pl.BlockSpec((1, XBC_WIDTH), const),
            pl.BlockSpec((1, DT_PAD), const),
            pl.BlockSpec((1, DT_PAD), const),
            pl.BlockSpec((1, SSM_WIDTH), const),
            pl.BlockSpec((1, SSM_WIDTH), const),
            pl.BlockSpec((DT_PAD, SSM_WIDTH), const),
        ],
        out_specs=[
            pl.BlockSpec((SSD_CHUNK, SSM_WIDTH), row),
            pl.BlockSpec((1, SSM_HEADS, SSM_HEADDIM, SSM_STATE), lambda b, c: (b, 0, 0, 0)),
            pl.BlockSpec((1, CONV_WIDTH - 1, XBC_WIDTH), lambda b, c: (b, 0, 0)),
        ],
        out_shape=[
            jax.ShapeDtypeStruct((batch * seq, SSM_WIDTH), BF16),
            jax.ShapeDtypeStruct((batch, SSM_HEADS, SSM_HEADDIM, SSM_STATE), F32),
            jax.ShapeDtypeStruct((batch, CONV_WIDTH - 1, XBC_WIDTH), F32),
        ],
        scratch_shapes=[
            pltpu.VMEM((SSM_HEADS, SSM_HEADDIM, SSM_STATE), F32),
            pltpu.VMEM((8 + SSD_CHUNK + 8, XBC_WIDTH), F32),
            pltpu.VMEM((SSD_CHUNK, SSM_WIDTH), F32),
            pltpu.VMEM((SSD_CHUNK, SSM_WIDTH), F32),
        ],
        compiler_params=pltpu.CompilerParams(
            dimension_semantics=("parallel", "arbitrary"), vmem_limit_bytes=VMEM_LIMIT),
        name="ssd_prompt",
    )(xbc, z, dt, conv_w, conv_b, dtb, alog, dsk, nw, eh)


def _attn_prompt_kernel(q_ref, g_ref, k_ref, v_ref, o_ref, kb_scr, vt_scr, km_scr, st_scr, *, k_sel):
    n = pl.program_id(2)
    nblk = kb_scr.shape[0]

    @pl.when(n == 0)
    def _():
        for j in range(nblk):
            kf = k_ref[MOBA_BLOCK * j:MOBA_BLOCK * (j + 1), :]
            kb_scr[j] = kf.astype(BF16)
            km_scr[j:j + 1, :] = jnp.mean(kf, axis=0, keepdims=True)
            vt = v_ref[MOBA_BLOCK * j:MOBA_BLOCK * (j + 1), :].T.astype(BF16)
            vt_scr[j] = jnp.concatenate([vt, jnp.ones((SUM_ROWS, MOBA_BLOCK), BF16)], axis=0)

    own = (n * Q_TILE) // MOBA_BLOCK
    qf = q_ref[...]
    qt = jnp.concatenate([qf[:, LANES * r:LANES * (r + 1)].T for r in range(Q_PER_KV)], axis=1)
    qtb = (qt * SOFTMAX_C).astype(BF16)
    n_cols = Q_PER_KV * Q_TILE

    sels = []
    if k_sel > 0:
        kh, kmid, _ = _split3(km_scr[...])
        qh, qmid, _ = _split3(qt)
        gate = _nn(kh, qh) + _nn(kh, qmid) + _nn(kmid, qh)
        rowf = lax.broadcasted_iota(jnp.int32, gate.shape, 0).astype(F32)
        gate = jnp.where(rowf < own.astype(F32), gate, NEG)
        for t in range(k_sel):
            m = jnp.max(gate, axis=0, keepdims=True)
            idx = jnp.min(jnp.where(gate == m, rowf, float(nblk)), axis=0, keepdims=True)
            sels.append(jnp.where(t < own, idx, -1.0))
            gate = jnp.where(rowf == idx, -jnp.inf, gate)

    def scores(j):
        return _nn(kb_scr[j], qtb)

    def chose(j):
        jf = lax.convert_element_type(j, F32)
        hit = sels[0] == jf
        for t in range(1, k_sel):
            hit = jnp.logical_or(hit, sels[t] == jf)
        return hit

    kpos = own * MOBA_BLOCK + lax.broadcasted_iota(jnp.int32, (MOBA_BLOCK, n_cols), 0)
    qpos = n * Q_TILE + (lax.broadcasted_iota(jnp.int32, (MOBA_BLOCK, n_cols), 1) & (Q_TILE - 1))
    st = jnp.where(kpos <= qpos, scores(own), NEG)
    m = jnp.max(st, axis=0, keepdims=True)
    acc = _nn(vt_scr[own], jnp.exp2(st - m).astype(BF16))

    n_groups = (own + ATTN_GROUP - 1) // ATTN_GROUP

    def produce(grp):
        mg = jnp.full((1, n_cols), NEG, F32)
        for u in range(ATTN_GROUP):
            j = grp * ATTN_GROUP + u
            st = scores(jnp.minimum(j, nblk - 1))
            st_scr[grp % 2, u] = st
            mg = jnp.maximum(mg, jnp.where(chose(j), jnp.max(st, axis=0, keepdims=True), NEG))
        return mg

    if k_sel > 0:
        def body(grp, carry):
            m, acc, mg = carry
            m_new = jnp.maximum(m, mg)
            alpha = jnp.exp2(m - m_new)
            acc = alpha * acc
            for u in range(ATTN_GROUP):
                j = grp * ATTN_GROUP + u
                p = jnp.exp2(st_scr[grp % 2, u] - jnp.where(chose(j), m_new, -NEG))
                acc = acc + _nn(vt_scr[jnp.minimum(j, nblk - 1)], p.astype(BF16))
            mg_next = produce(grp + 1)
            return m_new, acc, mg_next
        _, acc, _ = lax.fori_loop(0, n_groups, body, (m, acc, produce(0)))

    o = acc[:ATTN_HEADDIM] / acc[ATTN_HEADDIM:ATTN_HEADDIM + 1]
    gg = _silu(g_ref[...])
    for r in range(Q_PER_KV):
        o_ref[:, LANES * r:LANES * (r + 1)] = (
            o[:, Q_TILE * r:Q_TILE * (r + 1)].T * gg[:, LANES * r:LANES * (r + 1)]).astype(o_ref.dtype)


def _attn_prompt(q, g, k, v, batch, seq):
    nq = seq // Q_TILE
    nblk = seq // MOBA_BLOCK
    k_sel = min(MOBA_TOPK, nblk - 1)
    qmap = lambda b, h, n: (b * nq + n, h)
    kvmap = lambda b, h, n: (b, h)
    width = Q_PER_KV * ATTN_HEADDIM
    return pl.pallas_call(
        functools.partial(_attn_prompt_kernel, k_sel=k_sel),
        grid=(batch, KV_HEADS, nq),
        in_specs=[
            pl.BlockSpec((Q_TILE, width), qmap),
            pl.BlockSpec((Q_TILE, width), qmap),
            pl.BlockSpec((seq, ATTN_HEADDIM), kvmap),
            pl.BlockSpec((seq, ATTN_HEADDIM), kvmap),
        ],
        out_specs=pl.BlockSpec((Q_TILE, width), qmap),
        out_shape=jax.ShapeDtypeStruct((batch * seq, ATTN_WIDTH), BF16),
        scratch_shapes=[
            pltpu.VMEM((nblk, MOBA_BLOCK, ATTN_HEADDIM), BF16),
            pltpu.VMEM((nblk, ATTN_HEADDIM + SUM_ROWS, MOBA_BLOCK), BF16),
            pltpu.VMEM((nblk, ATTN_HEADDIM), F32),
            pltpu.VMEM((2, ATTN_GROUP, MOBA_BLOCK, Q_PER_KV * Q_TILE), F32),
        ],
        compiler_params=pltpu.CompilerParams(
            dimension_semantics=("parallel", "parallel", "arbitrary"),
            vmem_limit_bytes=VMEM_LIMIT),
        name="moba_prompt",
    )(q, g, k, v)


def _out_kernel(ys_ref, ya_ref, x_ref, w1_ref, w2_ref, nw_ref, o_ref):
    out = x_ref[...] + _nn(ys_ref[...].astype(BF16), w1_ref[...])
    out = out + _nn(ya_ref[...].astype(BF16), w2_ref[...])
    o_ref[...] = _rmsnorm(out, nw_ref[...])


def _out_proj(ys, ya, x, w1, w2, nw, tm):
    t = x.shape[0]
    row = lambda i: (i, 0)
    const = lambda i: (0, 0)
    return pl.pallas_call(
        _out_kernel,
        grid=(t // tm,),
        in_specs=[
            pl.BlockSpec((tm, SSM_WIDTH), row),
            pl.BlockSpec((tm, ATTN_WIDTH), row),
            pl.BlockSpec((tm, D_MODEL), row),
            pl.BlockSpec((SSM_WIDTH, D_MODEL), const),
            pl.BlockSpec((ATTN_WIDTH, D_MODEL), const),
            pl.BlockSpec((1, D_MODEL), const),
        ],
        out_specs=pl.BlockSpec((tm, D_MODEL), row),
        out_shape=jax.ShapeDtypeStruct((t, D_MODEL), F32),
        compiler_params=pltpu.CompilerParams(
            dimension_semantics=("parallel",), vmem_limit_bytes=VMEM_LIMIT),
        name="out_proj",
    )(ys, ya, x, w1, w2, nw)


def _ssd_sample_kernel(xbc_ref, cst_ref, z_ref, dt_ref, h0_ref, cw_ref, cb_ref, dtb_ref, alog_ref,
                       dsk_ref, nw_ref, eh_ref, eg_ref,
                       y_ref, hnew_ref, cnew_ref, *, steps):
    tb = SAMPLE_TILE
    hist = CONV_WIDTH - 1
    rows = [cst_ref[:, j, :] for j in range(hist)] + [xbc_ref[:, t, :] for t in range(steps)]
    for j in range(hist):
        cnew_ref[:, j, :] = rows[steps + j]

    a_neg = -jnp.exp(alog_ref[...])
    u, dts, acs = [], [], []
    run = None
    for t in range(steps):
        conv = cb_ref[...]
        for i in range(CONV_WIDTH):
            conv = conv + rows[t + i] * cw_ref[i:i + 1, :]
        u.append(_silu(conv))
        dt = _softplus(dt_ref[:, t, :] + dtb_ref[...])
        dts.append(dt)
        run = dt * a_neg if run is None else run + dt * a_neg
        acs.append(run)
    uu = jnp.concatenate(u, axis=0)
    xs = uu[:, :SSM_WIDTH]
    bm = uu[:, SSM_WIDTH:SSM_WIDTH + SSM_GROUPS * SSM_STATE]
    cm = uu[:, SSM_WIDTH + SSM_GROUPS * SSM_STATE:]
    dt_all = jnp.concatenate(dts, axis=0)
    ac_all = jnp.concatenate(acs, axis=0)
    a_last = jnp.concatenate([acs[-1]] * steps, axis=0)

    eh = eh_ref[...]
    xdt = xs * _nn_exact_rhs(dt_all, eh)
    xw = xs * _nn_exact_rhs(dt_all * jnp.exp(a_last - ac_all), eh)
    ea = _nn_exact_rhs(jnp.exp(ac_all), eh)

    eg = eg_ref[...]
    ydiag = []
    for t in range(steps):
        acc = None
        for j in range(t + 1):
            cbx = _nn_exact_rhs(cm[tb * t:tb * (t + 1)] * bm[tb * j:tb * (j + 1)], eg)
            lmx = _nn_exact_rhs(jnp.exp(acs[t] - acs[j]), eh)
            term = cbx * lmx * xdt[tb * j:tb * (j + 1)]
            acc = term if acc is None else acc + term
        ydiag.append(acc)
    y = jnp.concatenate(ydiag, axis=0) + dsk_ref[...] * xs

    rowb = lax.broadcasted_iota(jnp.int32, (steps * tb, 1), 0) % tb
    cmb = cm.astype(BF16)
    gw = HEADS_PER_GROUP * SSM_HEADDIM
    yoff = [None] * SSM_GROUPS
    for b in range(tb):
        mine = rowb == b
        for g in range(SSM_GROUPS):
            h0g = h0_ref[b, HEADS_PER_GROUP * g:HEADS_PER_GROUP * (g + 1)].reshape(gw, SSM_STATE)
            part = jnp.where(mine, _nt(cmb[:, SSM_STATE * g:SSM_STATE * (g + 1)], h0g.astype(BF16)), 0.0)
            yoff[g] = part if yoff[g] is None else yoff[g] + part
            xwb = jnp.where(mine, xw[:, gw * g:gw * (g + 1)], 0.0)
            xh, xm, _ = _split3(xwb)
            bh, bmid, _ = _split3(bm[:, SSM_STATE * g:SSM_STATE * (g + 1)])
            st = _tn(xh, bh) + _tn(xh, bmid) + _tn(xm, bh)
            for r in range(HEADS_PER_GROUP):
                h = HEADS_PER_GROUP * g + r
                dec = jnp.exp(acs[-1][b:b + 1, h:h + 1])
                hnew_ref[b, h] = (h0g[SSM_HEADDIM * r:SSM_HEADDIM * (r + 1)] * dec
                                  + st[SSM_HEADDIM * r:SSM_HEADDIM * (r + 1)])
    y = y + jnp.concatenate(yoff, axis=1) * ea

    for t in range(steps):
        yt = y[tb * t:tb * (t + 1)] * _silu(z_ref[:, t, :])
        y_ref[:, t, :] = _rmsnorm(yt, nw_ref[...])


def _ssd_sample(xbc, cst, z, dt, h0, conv_w, conv_b, dtb, alog, dsk, nw, eh, eg):
    nb, steps = xbc.shape[0], xbc.shape[1]
    tb = SAMPLE_TILE
    c2 = lambda i: (0, 0)
    b3 = lambda i: (i, 0, 0)
    b4 = lambda i: (i, 0, 0, 0)
    hist = CONV_WIDTH - 1
    return pl.pallas_call(
        functools.partial(_ssd_sample_kernel, steps=steps),
        grid=(nb // tb,),
        in_specs=[
            pl.BlockSpec((tb, steps, XBC_WIDTH), b3),
            pl.BlockSpec((tb, hist, XBC_WIDTH), b3),
            pl.BlockSpec((tb, steps, SSM_WIDTH), b3),
            pl.BlockSpec((tb, steps, DT_PAD), b3),
            pl.BlockSpec((tb, SSM_HEADS, SSM_HEADDIM, SSM_STATE), b4),
            pl.BlockSpec((CONV_WIDTH, XBC_WIDTH), c2),
            pl.BlockSpec((1, XBC_WIDTH), c2),
            pl.BlockSpec((1, DT_PAD), c2),
            pl.BlockSpec((1, DT_PAD), c2),
            pl.BlockSpec((1, SSM_WIDTH), c2),
            pl.BlockSpec((1, SSM_WIDTH), c2),
            pl.BlockSpec((DT_PAD, SSM_WIDTH), c2),
            pl.BlockSpec((SSM_GROUPS * SSM_STATE, SSM_WIDTH), c2),
        ],
        out_specs=[
            pl.BlockSpec((tb, steps, SSM_WIDTH), b3),
            pl.BlockSpec((tb, SSM_HEADS, SSM_HEADDIM, SSM_STATE), b4),
            pl.BlockSpec((tb, hist, XBC_WIDTH), b3),
        ],
        out_shape=[
            jax.ShapeDtypeStruct((nb, steps, SSM_WIDTH), F32),
            jax.ShapeDtypeStruct((nb, SSM_HEADS, SSM_HEADDIM, SSM_STATE), F32),
            jax.ShapeDtypeStruct((nb, hist, XBC_WIDTH), F32),
        ],
        compiler_params=pltpu.CompilerParams(
            dimension_semantics=("parallel",), vmem_limit_bytes=VMEM_LIMIT),
        name="ssd_sample",
    )(xbc, cst, z, dt, h0, conv_w, conv_b, dtb, alog, dsk, nw, eh, eg)


def _attn_sample_kernel(pt_ref, q_ref, g_ref, kn_ref, vn_ref, ck_ref, cv_ref, o_ref,
                        kbuf, vbuf, km_scr, w_scr, idx_v, idx_s, o_scr, sems, *, n_pages, steps):
    b = pl.program_id(0)
    slot = b % 2
    n_full = n_pages // PAGES_PER_BLOCK
    page_rows = PAGE_SIZE * KV_HEADS
    n_cols = ATTN_HEADS * steps
    per_kv = Q_PER_KV * steps

    def k_copy(seq, sl, p):
        return pltpu.make_async_copy(ck_ref.at[pt_ref[seq * n_pages + p]], kbuf.at[sl, p], sems.at[sl])

    def v_copy(seq, p):
        return pltpu.make_async_copy(cv_ref.at[pt_ref[seq * n_pages + p]], vbuf.at[p], sems.at[2])

    @pl.when(b == 0)
    def _():
        for p in range(n_pages):
            k_copy(0, 0, p).start(priority=p % 2)
        for p in range(n_pages):
            v_copy(0, p).start(priority=p % 2)

    for p in range(n_pages):
        k_copy(b, slot, p).wait()

    km_scr[...] = jnp.zeros_like(km_scr)
    for blk in range(n_full):
        tot = None
        for pp in range(PAGES_PER_BLOCK):
            page = kbuf[slot, PAGES_PER_BLOCK * blk + pp]
            part = page.reshape(page_rows // 8, 8, LANES).sum(axis=0)
            tot = part if tot is None else tot + part
        km_scr[KV_HEADS * blk:KV_HEADS * (blk + 1), :] = (
            (tot[:KV_HEADS] + tot[KV_HEADS:]) * (1.0 / MOBA_BLOCK))

    qf = q_ref[0]
    q_all = jnp.concatenate([qf[:, LANES * h:LANES * (h + 1)] for h in range(ATTN_HEADS)], axis=0)
    q_pad = jnp.concatenate([q_all, jnp.zeros((LANES - n_cols, LANES), F32)], axis=0)

    gate = _nt_x3(q_pad, km_scr[...])
    lane = lax.broadcasted_iota(jnp.int32, gate.shape, 1)
    rowc = lax.broadcasted_iota(jnp.int32, gate.shape, 0)
    ok = jnp.logical_and(lane % KV_HEADS == rowc // per_kv, lane < KV_HEADS * n_full)
    gate = jnp.where(ok, gate, -jnp.inf)
    lanef = lane.astype(F32)
    chosen = jnp.zeros(gate.shape, jnp.int32)
    for t in range(MOBA_TOPK):
        m = jnp.max(gate, axis=-1, keepdims=True)
        idx = jnp.min(jnp.where(gate == m, lanef, float(4 * LANES)), axis=-1, keepdims=True)
        chosen = jnp.where(lane == t, idx.astype(jnp.int32) // KV_HEADS, chosen)
        gate = jnp.where(lanef == idx, -jnp.inf, gate)
    idx_v[...] = chosen
    to_smem = pltpu.make_async_copy(idx_v, idx_s, sems.at[3])
    to_smem.start()

    q_t = q_pad.T
    for c in range(n_cols):
        w_scr[c] = jnp.broadcast_to(q_t[:, c:c + 1], (ATTN_HEADDIM, LANES)).astype(BF16)

    to_smem.wait()
    for p in range(n_pages):
        v_copy(b, p).wait()

    @pl.when(b + 1 < pl.num_programs(0))
    def _():
        for p in range(n_pages):
            k_copy(b + 1, 1 - slot, p).start(priority=p % 2)

    sub8 = lax.broadcasted_iota(jnp.int32, (8, LANES), 0)

    def k_tile(blk, g):
        return jnp.concatenate(
            [kbuf[slot, PAGES_PER_BLOCK * blk + pp, pl.ds(g, PAGE_SIZE, stride=KV_HEADS), :]
             for pp in range(PAGES_PER_BLOCK)], axis=0)

    def v_tile(blk, g):
        return jnp.concatenate(
            [vbuf[PAGES_PER_BLOCK * blk + pp, pl.ds(g, PAGE_SIZE, stride=KV_HEADS), :]
             for pp in range(PAGES_PER_BLOCK)], axis=0)

    def column(c, i, g, knb, vn):
        w = w_scr[c]
        s_own = jnp.where(sub8 <= i % steps, _nn(knb, w) * SOFTMAX_C, NEG)
        mx = jnp.max(s_own, axis=0, keepdims=True)
        blks = [idx_s[c, t] for t in range(MOBA_TOPK)]
        scores = []
        for blk in blks:
            s = _nn(k_tile(blk, g).astype(BF16), w) * SOFTMAX_C
            scores.append(s)
            mx = jnp.maximum(mx, jnp.max(s, axis=0, keepdims=True))
        p = jnp.exp2(s_own - mx)
        l = jnp.sum(p, axis=0, keepdims=True)
        acc = jnp.sum(p * vn, axis=0, keepdims=True)
        for blk, s in zip(blks, scores):
            p = jnp.exp2(s - mx)
            l = l + jnp.sum(p, axis=0, keepdims=True)
            acc = acc + jnp.sum(p * v_tile(blk, g), axis=0, keepdims=True)
        return acc / l

    for g in range(KV_HEADS):
        pad = jnp.zeros((8 - steps, LANES), F32)
        kn = jnp.concatenate([kn_ref[0][:, LANES * g:LANES * (g + 1)], pad], axis=0)
        vn = jnp.concatenate([vn_ref[0][:, LANES * g:LANES * (g + 1)], pad], axis=0)
        knb = kn.astype(BF16)

        def pair_body(ip, _, g=g, knb=knb, vn=vn):
            for u in range(2):
                i = 2 * ip + u
                c = g * per_kv + i
                o_scr[pl.ds(c, 1), :] = column(c, i, g, knb, vn)
            return 0

        lax.fori_loop(0, per_kv // 2, pair_body, 0)

    @pl.when(b + 1 < pl.num_programs(0))
    def _():
        for p in range(n_pages):
            v_copy(b + 1, p).start(priority=p % 2)

    gg = _silu(g_ref[0])
    for h in range(ATTN_HEADS):
        o_ref[0, :, LANES * h:LANES * (h + 1)] = (
            o_scr[steps * h:steps * (h + 1), :] * gg[:, LANES * h:LANES * (h + 1)])


def _attn_sample(page_table, q, g, k_new, v_new, cache_k, cache_v):
    nb, steps = q.shape[0], q.shape[1]
    n_pages = page_table.shape[1]
    n_pool = cache_k.shape[0]
    page_rows = PAGE_SIZE * KV_HEADS
    ck = cache_k.reshape(n_pool, page_rows, ATTN_HEADDIM)
    cv = cache_v.reshape(n_pool, page_rows, ATTN_HEADDIM)
    b3 = lambda i, pt: (i, 0, 0)
    grid_spec = pltpu.PrefetchScalarGridSpec(
        num_scalar_prefetch=1,
        grid=(nb,),
        in_specs=[
            pl.BlockSpec((1, steps, ATTN_WIDTH), b3),
            pl.BlockSpec((1, steps, ATTN_WIDTH), b3),
            pl.BlockSpec((1, steps, KV_WIDTH), b3),
            pl.BlockSpec((1, steps, KV_WIDTH), b3),
            pl.BlockSpec(memory_space=pl.ANY),
            pl.BlockSpec(memory_space=pl.ANY),
        ],
        out_specs=pl.BlockSpec((1, steps, ATTN_WIDTH), b3),
        scratch_shapes=[
            pltpu.VMEM((2, n_pages, page_rows, ATTN_HEADDIM), F32),
            pltpu.VMEM((n_pages, page_rows, ATTN_HEADDIM), F32),
            pltpu.VMEM((LANES, ATTN_HEADDIM), F32),
            pltpu.VMEM((ATTN_HEADS * steps, ATTN_HEADDIM, LANES), BF16),
            pltpu.VMEM((LANES, LANES), jnp.int32),
            pltpu.SMEM((LANES, LANES), jnp.int32),
            pltpu.VMEM((ATTN_HEADS * steps, ATTN_HEADDIM), F32),
            pltpu.SemaphoreType.DMA((4,)),
        ],
    )
    return pl.pallas_call(
        functools.partial(_attn_sample_kernel, n_pages=n_pages, steps=steps),
        grid_spec=grid_spec,
        out_shape=jax.ShapeDtypeStruct((nb, steps, ATTN_WIDTH), F32),
        compiler_params=pltpu.CompilerParams(
            dimension_semantics=("arbitrary",), vmem_limit_bytes=VMEM_LIMIT),
        name="moba_sample",
    )(page_table.reshape(-1), q, g, k_new, v_new, ck, cv)


def _rope_tables(pos):
    inv = 1.0 / (ROPE_THETA ** (jnp.arange(0, ROT_DIM, 2, dtype=F32) / ROT_DIM))
    ang = pos.astype(F32)[:, None] * inv[None, :]
    cos, sin = jnp.cos(ang), jnp.sin(ang)
    n = pos.shape[0]
    rest = LANES - ROT_DIM
    cosf = jnp.concatenate([cos, cos, jnp.ones((n, rest), F32)], axis=1)
    sinf = jnp.concatenate([-sin, sin, jnp.zeros((n, rest), F32)], axis=1)
    return cosf, sinf


def _row_tile(t, cap):
    tm = cap
    while t % tm:
        tm //= 2
    return tm


def kernel(x_prompt, x_sample, cache_k, cache_v, state_ssm, state_conv, page_table,
           norm_w, w_in, conv_w, conv_b, dt_bias, a_log, d_skip, ssm_norm_w, w_out, final_norm_w):
    depth = w_in.shape[0]
    assert depth == 1
    bp, seq, _ = x_prompt.shape
    nb, steps, _ = x_sample.shape
    n_pages = page_table.shape[1]
    past = n_pages * PAGE_SIZE
    assert seq % MOBA_BLOCK == 0 and past % MOBA_BLOCK == 0 and steps <= 8
    assert n_pages // PAGES_PER_BLOCK >= MOBA_TOPK and nb % SAMPLE_TILE == 0
    assert KV_HEADS * (n_pages // PAGES_PER_BLOCK) <= LANES

    w = w_in[0]
    c_dt = SSM_WIDTH + XBC_WIDTH
    wa = w[:, :c_dt].astype(BF16)
    wb = w[:, c_dt + SSM_HEADS:].astype(BF16)
    wdt = jnp.pad(w[:, c_dt:c_dt + SSM_HEADS], ((0, 0), (0, DT_PAD - SSM_HEADS))).astype(BF16)
    nw = norm_w[0].reshape(1, D_MODEL)
    pad_h = lambda v: jnp.concatenate([v, jnp.zeros((DT_PAD - SSM_HEADS,), F32)]).reshape(1, DT_PAD)
    dtb, alog = pad_h(dt_bias[0]), pad_h(a_log[0])
    dsk = jnp.repeat(d_skip[0], SSM_HEADDIM).reshape(1, SSM_WIDTH)
    snw = ssm_norm_w[0].reshape(1, SSM_WIDTH)
    cb = conv_b[0].reshape(1, XBC_WIDTH)
    w1 = w_out[0][:SSM_WIDTH].astype(BF16)
    w2 = w_out[0][SSM_WIDTH:].astype(BF16)
    fnw = final_norm_w.reshape(1, D_MODEL)

    heads = jnp.arange(DT_PAD)[:, None]
    lanes = jnp.arange(SSM_WIDTH)[None, :]
    eh = (lanes // SSM_HEADDIM == heads).astype(BF16)
    rows_g = jnp.arange(SSM_GROUPS * SSM_STATE)[:, None] // SSM_STATE
    eg = (lanes // (HEADS_PER_GROUP * SSM_HEADDIM) == rows_g).astype(BF16)

    xp = x_prompt.reshape(bp * seq, D_MODEL)
    tm_p = _row_tile(seq, 256)
    cos_p, sin_p = _rope_tables(jnp.arange(seq))
    z, xbc, q, k, v, g, dt, k_rows, v_rows = _proj(xp, nw, wa, wb, wdt, cos_p, sin_p, tm_p)
    y_ssm, h_p, c_p = _ssd_prompt(xbc, z, dt, conv_w[0], cb, dtb, alog, dsk, snw, eh, bp, seq)
    y_attn = _attn_prompt(q, g, k, v, bp, seq)
    y_p = _out_proj(y_ssm, y_attn, xp, w1, w2, fnw, _row_tile(bp * seq, 512))

    ts = nb * steps
    xs = x_sample.reshape(ts, D_MODEL)
    tm_s = _row_tile(ts, 256)
    cos_s, sin_s = _rope_tables(past + jnp.arange(steps))
    cos_s, sin_s = jnp.tile(cos_s, (tm_s // steps, 1)), jnp.tile(sin_s, (tm_s // steps, 1))
    zs, xbcs, qs, ks, vs, gs, dts, ks_rows, vs_rows = _proj(xs, nw, wa, wb, wdt, cos_s, sin_s, tm_s)
    r3 = lambda a: a.reshape(nb, steps, a.shape[-1])
    ys_ssm, h_s, c_s = _ssd_sample(r3(xbcs), state_conv[0], r3(zs), r3(dts), state_ssm[0],
                                   conv_w[0], cb, dtb, alog, dsk, snw, eh, eg)
    ys_attn = _attn_sample(page_table, r3(qs), r3(gs), r3(ks), r3(vs), cache_k[0], cache_v[0])
    y_s = _out_proj(ys_ssm.reshape(ts, SSM_WIDTH), ys_attn.reshape(ts, ATTN_WIDTH), xs, w1, w2, fnw,
                    _row_tile(ts, 512))

    kv5 = lambda a, n, s: a.reshape(1, n, s, KV_HEADS, ATTN_HEADDIM)
    return (y_p.reshape(bp, seq, D_MODEL), y_s.reshape(nb, steps, D_MODEL),
            kv5(k_rows, bp, seq), kv5(v_rows, bp, seq), kv5(ks_rows, nb, steps), kv5(vs_rows, nb, steps),
            h_p[None], c_p[None], h_s[None], c_s[None])
```

```python
import functools

import jax
import jax.numpy as jnp
from jax import lax
from jax.experimental import pallas as pl
from jax.experimental.pallas import tpu as pltpu

F32 = jnp.float32
BF16 = jnp.bfloat16

D_MODEL = 1024
SSM_WIDTH = 1024
SSM_HEADDIM = 64
SSM_HEADS = SSM_WIDTH // SSM_HEADDIM
SSM_GROUPS = 4
HEADS_PER_GROUP = SSM_HEADS // SSM_GROUPS
SSM_STATE = 128
CONV_WIDTH = 4
SSD_CHUNK = 128
XBC_WIDTH = SSM_WIDTH + 2 * SSM_GROUPS * SSM_STATE
ATTN_HEADDIM = 128
ATTN_HEADS = 8
KV_HEADS = 4
Q_PER_KV = ATTN_HEADS // KV_HEADS
ATTN_WIDTH = ATTN_HEADS * ATTN_HEADDIM
KV_WIDTH = KV_HEADS * ATTN_HEADDIM
MOBA_BLOCK = 256
MOBA_TOPK = 3
Q_TILE = 256
ROT_DIM = 32
ROPE_THETA = 500000.0
PAGE_SIZE = 128
PAGES_PER_BLOCK = MOBA_BLOCK // PAGE_SIZE
EPS = 1e-6
NEG = -1e30
SOFTMAX_C = ATTN_HEADDIM ** -0.5 * 1.4426950408889634
LANES = 128
DT_PAD = LANES
VMEM_LIMIT = 56 * 1024 * 1024
SAMPLE_TILE = 8
ATTN_GROUP = 4
SUM_ROWS = 16


def _nt(a, b):
    return lax.dot_general(a, b, (((1,), (1,)), ((), ())), preferred_element_type=F32)


def _tn(a, b):
    return lax.dot_general(a, b, (((0,), (0,)), ((), ())), preferred_element_type=F32)


def _nn(a, b):
    return jnp.dot(a, b, preferred_element_type=F32)


def _split3(x):
    hi = x.astype(BF16)
    r1 = x - hi.astype(F32)
    mid = r1.astype(BF16)
    lo = (r1 - mid.astype(F32)).astype(BF16)
    return hi, mid, lo


def _nn_exact_rhs(x, e):
    hi, mid, lo = _split3(x)
    return _nn(hi, e) + _nn(mid, e) + _nn(lo, e)


def _expand2(x, e):
    hi = x.astype(BF16)
    mid = (x - hi.astype(F32)).astype(BF16)
    return _nn(hi, e) + _nn(mid, e)


def _nt_x3(a, b):
    ah, am, _ = _split3(a)
    bh, bm, _ = _split3(b)
    return _nt(ah, bh) + _nt(ah, bm) + _nt(am, bh)


def _silu(x):
    return (0.5 * x) * (1.0 + jnp.tanh(0.5 * x))


def _softplus(x):
    return jnp.maximum(x, 0.0) + jnp.log1p(jnp.exp(-jnp.abs(x)))


def _rmsnorm(x, w):
    ms = jnp.mean(x * x, axis=-1, keepdims=True)
    return x * lax.rsqrt(ms + EPS) * w


def _proj_kernel(x_ref, nw_ref, wa_ref, wb_ref, wdt_ref, cos_ref, sin_ref,
                 z_ref, xbc_ref, q_ref, k_ref, v_ref, g_ref, dt_ref, krows_ref, vrows_ref):
    h = _rmsnorm(x_ref[...], nw_ref[...]).astype(BF16)
    tm = x_ref.shape[0]

    z_ref[...] = _nn(h, wa_ref[:, :SSM_WIDTH])
    xbc_ref[...] = _nn(h, wa_ref[:, SSM_WIDTH:])
    o = 0
    q = _nn(h, wb_ref[:, o:o + ATTN_WIDTH]); o += ATTN_WIDTH
    k = _nn(h, wb_ref[:, o:o + KV_WIDTH]); o += KV_WIDTH
    v = _nn(h, wb_ref[:, o:o + KV_WIDTH]); o += KV_WIDTH
    v_ref[...] = v
    for hh in range(KV_HEADS):
        vrows_ref[pl.ds(hh, tm, stride=KV_HEADS), :] = v[:, LANES * hh:LANES * (hh + 1)]
    g_ref[...] = _nn(h, wb_ref[:, o:o + ATTN_WIDTH])
    dt_ref[...] = _nn(h, wdt_ref[...])

    cosf = cos_ref[...]
    sinf = sin_ref[...]
    first_half = lax.broadcasted_iota(jnp.int32, cosf.shape, 1) < ROT_DIM // 2

    def rope(t, n_heads, out_ref, rows_ref=None):
        for hh in range(n_heads):
            th = t[:, LANES * hh:LANES * (hh + 1)]
            partner = jnp.where(first_half,
                                pltpu.roll(th, LANES - ROT_DIM // 2, 1),
                                pltpu.roll(th, ROT_DIM // 2, 1))
            roped = th * cosf + partner * sinf
            out_ref[:, LANES * hh:LANES * (hh + 1)] = roped
            if rows_ref is not None:
                rows_ref[pl.ds(hh, tm, stride=n_heads), :] = roped

    rope(q, ATTN_HEADS, q_ref)
    rope(k, KV_HEADS, k_ref, krows_ref)


def _proj(x, norm_w, wa, wb, wdt, cosf, sinf, tm):
    t = x.shape[0]
    n_tab = cosf.shape[0] // tm
    row = lambda i: (i, 0)
    const = lambda i: (0, 0)
    widths = (SSM_WIDTH, XBC_WIDTH, ATTN_WIDTH, KV_WIDTH, KV_WIDTH, ATTN_WIDTH, DT_PAD)
    return pl.pallas_call(
        _proj_kernel,
        grid=(t // tm,),
        in_specs=[
            pl.BlockSpec((tm, D_MODEL), row),
            pl.BlockSpec((1, D_MODEL), const),
            pl.BlockSpec(wa.shape, const, pipeline_mode=pl.Buffered(1)),
            pl.BlockSpec(wb.shape, const, pipeline_mode=pl.Buffered(1)),
            pl.BlockSpec(wdt.shape, const, pipeline_mode=pl.Buffered(1)),
            pl.BlockSpec((tm, LANES), lambda i: (i % n_tab, 0)),
            pl.BlockSpec((tm, LANES), lambda i: (i % n_tab, 0)),
        ],
        out_specs=([pl.BlockSpec((tm, w), row) for w in widths]
                   + [pl.BlockSpec((tm * KV_HEADS, ATTN_HEADDIM), row)] * 2),
        out_shape=([jax.ShapeDtypeStruct((t, w), F32) for w in widths]
                   + [jax.ShapeDtypeStruct((t * KV_HEADS, ATTN_HEADDIM), F32)] * 2),
        compiler_params=pltpu.CompilerParams(
            dimension_semantics=("parallel",), vmem_limit_bytes=VMEM_LIMIT),
        name="in_proj",
    )(x, norm_w, wa, wb, wdt, cosf, sinf)


def _ssd_prompt_kernel(xbc_ref, z_ref, dt_ref, cw_ref, cb_ref, dtb_ref, alog_ref, dsk_ref, nw_ref, eh_ref,
                       y_ref, hfin_ref, cfin_ref, h_scr, xp_scr, yd_scr, yo_scr):
    c = pl.program_id(1)
    l = SSD_CHUNK
    hist = CONV_WIDTH - 1

    @pl.when(c == 0)
    def _():
        h_scr[...] = jnp.zeros_like(h_scr)
        xp_scr[0:8, :] = jnp.zeros((8, XBC_WIDTH), F32)

    xp_scr[8:8 + l, :] = xbc_ref[...]
    conv = cb_ref[...]
    for i in range(CONV_WIDTH):
        conv = conv + xp_scr[8 - hist + i:8 - hist + i + l, :] * cw_ref[i:i + 1, :]
    xp_scr[0:8, :] = xp_scr[l:l + 8, :]
    u = _silu(conv)
    xs = u[:, :SSM_WIDTH]

    dt = _softplus(dt_ref[...] + dtb_ref[...])
    a_neg = -jnp.exp(alog_ref[...])
    row = lax.broadcasted_iota(jnp.int32, (l, l), 0)
    col = lax.broadcasted_iota(jnp.int32, (l, l), 1)
    causal = col <= row
    tri = jnp.where(causal, 1.0, 0.0).astype(BF16)
    hi, mid, lo3 = _split3(dt * a_neg)
    acum = _nn(tri, hi) + _nn(tri, mid) + _nn(tri, lo3)
    acum_t = acum.T
    a_last = acum[l - 1:l, :]
    chunk_decay = jnp.exp(a_last)

    eh = eh_ref[...]
    xdt = xs * _expand2(dt, eh)
    xw = (xdt * _expand2(jnp.exp(a_last - acum), eh)).astype(BF16)
    xdtb = xdt.astype(BF16)
    upper = lax.broadcasted_iota(jnp.int32, (l, LANES), 1) < SSM_HEADDIM
    top_rows = lax.broadcasted_iota(jnp.int32, (2 * SSM_HEADDIM, SSM_STATE), 0) < SSM_HEADDIM
    zero = jnp.zeros((l, LANES), BF16)

    for g in range(SSM_GROUPS):
        b0 = SSM_WIDTH + SSM_STATE * g
        c0 = SSM_WIDTH + SSM_GROUPS * SSM_STATE + SSM_STATE * g
        bb = u[:, b0:b0 + SSM_STATE].astype(BF16)
        cc = u[:, c0:c0 + SSM_STATE].astype(BF16)
        cb = _nt(cc, bb)
        for r in range(0, HEADS_PER_GROUP, 2):
            h = g * HEADS_PER_GROUP + r
            p0 = SSM_HEADDIM * h
            ms = []
            for hh in (h, h + 1):
                seg = acum[:, hh:hh + 1] - acum_t[hh:hh + 1, :]
                ms.append((cb * jnp.exp(jnp.where(causal, seg, NEG))).astype(BF16))
            x2 = xdtb[:, p0:p0 + LANES]
            x_blockdiag = jnp.concatenate([jnp.where(upper, x2, zero), jnp.where(upper, zero, x2)], axis=0)
            yd_scr[:, p0:p0 + LANES] = _nn(jnp.concatenate(ms, axis=1), x_blockdiag)
            hs = h_scr[h:h + 2].reshape(2 * SSM_HEADDIM, SSM_STATE)
            yo_scr[:, p0:p0 + LANES] = _nt(cc, hs.astype(BF16))
            st = _tn(xw[:, p0:p0 + LANES], bb)
            decay = jnp.where(top_rows, chunk_decay[:, h:h + 1], chunk_decay[:, h + 1:h + 2])
            h_scr[h:h + 2] = (hs * decay + st).reshape(2, SSM_HEADDIM, SSM_STATE)

    y = yd_scr[...] + yo_scr[...] * _expand2(jnp.exp(acum), eh) + dsk_ref[...] * xs
    y_ref[...] = _rmsnorm(y * _silu(z_ref[...]), nw_ref[...]).astype(y_ref.dtype)

    @pl.when(c == pl.num_programs(1) - 1)
    def _():
        hfin_ref[0] = h_scr[...]
        cfin_ref[0] = xp_scr[8 + l - hist:8 + l, :]


def _ssd_prompt(xbc, z, dt, conv_w, conv_b, dtb, alog, dsk, nw, eh, batch, seq):
    nc = seq // SSD_CHUNK
    row = lambda b, c: (b * nc + c, 0)
    const = lambda b, c: (0, 0)
    return pl.pallas_call(
        _ssd_prompt_kernel,
        grid=(batch, nc),
        in_specs=[
            pl.BlockSpec((SSD_CHUNK, XBC_WIDTH), row),
            pl.BlockSpec((SSD_CHUNK, SSM_WIDTH), row),
            pl.BlockSpec((SSD_CHUNK, DT_PAD), row),
            pl.BlockSpec((CONV_WIDTH, XBC_WIDTH), const),
            pl.BlockSpec((1, XBC_WIDTH), const),
            pl.BlockSpec((1, DT_PAD), const),
            pl.BlockSpec((1, DT_PAD), const),
            pl.BlockSpec((1, SSM_WIDTH), const),
            pl.BlockSpec((1, SSM_WIDTH), const),
            pl.BlockSpec((DT_PAD, SSM_WIDTH), const),
        ],
        out_specs=[
            pl.BlockSpec((SSD_CHUNK, SSM_WIDTH), row),
            pl.BlockSpec((1, SSM_HEADS, SSM_HEADDIM, SSM_STATE), lambda b, c: (b, 0, 0, 0)),
            pl.BlockSpec((1, CONV_WIDTH - 1, XBC_WIDTH), lambda b, c: (b, 0, 0)),
        ],
        out_shape=[
            jax.ShapeDtypeStruct((batch * seq, SSM_WIDTH), BF16),
            jax.ShapeDtypeStruct((batch, SSM_HEADS, SSM_HEADDIM, SSM_STATE), F32),
            jax.ShapeDtypeStruct((batch, CONV_WIDTH - 1, XBC_WIDTH), F32),
        ],
        scratch_shapes=[
            pltpu.VMEM((SSM_HEADS, SSM_HEADDIM, SSM_STATE), F32),
            pltpu.VMEM((8 + SSD_CHUNK + 8, XBC_WIDTH), F32),
            pltpu.VMEM((SSD_CHUNK, SSM_WIDTH), F32),
            pltpu.VMEM((SSD_CHUNK, SSM_WIDTH), F32),
        ],
        compiler_params=pltpu.CompilerParams(
            dimension_semantics=("parallel", "arbitrary"), vmem_limit_bytes=VMEM_LIMIT),
        name="ssd_prompt",
    )(xbc, z, dt, conv_w, conv_b, dtb, alog, dsk, nw, eh)


def _attn_prompt_kernel(q_ref, g_ref, k_ref, v_ref, o_ref, kb_scr, vt_scr, km_scr, st_scr, *, k_sel):
    n = pl.program_id(2)
    nblk = kb_scr.shape[0]

    @pl.when(n == 0)
    def _():
        for j in range(nblk):
            kf = k_ref[MOBA_BLOCK * j:MOBA_BLOCK * (j + 1), :]
            kb_scr[j] = kf.astype(BF16)
            km_scr[j:j + 1, :] = jnp.mean(kf, axis=0, keepdims=True)
            vt = v_ref[MOBA_BLOCK * j:MOBA_BLOCK * (j + 1), :].T.astype(BF16)
            vt_scr[j] = jnp.concatenate([vt, jnp.ones((SUM_ROWS, MOBA_BLOCK), BF16)], axis=0)

    own = (n * Q_TILE) // MOBA_BLOCK
    qf = q_ref[...]
    qt = jnp.concatenate([qf[:, LANES * r:LANES * (r + 1)].T for r in range(Q_PER_KV)], axis=1)
    qtb = (qt * SOFTMAX_C).astype(BF16)
    n_cols = Q_PER_KV * Q_TILE

    sels = []
    if k_sel > 0:
        kh, kmid, _ = _split3(km_scr[...])
        qh, qmid, _ = _split3(qt)
        gate = _nn(kh, qh) + _nn(kh, qmid) + _nn(kmid, qh)
        rowf = lax.broadcasted_iota(jnp.int32, gate.shape, 0).astype(F32)
        gate = jnp.where(rowf < own.astype(F32), gate, NEG)
        for t in range(k_sel):
            m = jnp.max(gate, axis=0, keepdims=True)
            idx = jnp.min(jnp.where(gate == m, rowf, float(nblk)), axis=0, keepdims=True)
            sels.append(jnp.where(t < own, idx, -1.0))
            gate = jnp.where(rowf == idx, -jnp.inf, gate)

    def scores(j):
        return _nn(kb_scr[j], qtb)

    def chose(j):
        jf = lax.convert_element_type(j, F32)
        hit = sels[0] == jf
        for t in range(1, k_sel):
            hit = jnp.logical_or(hit, sels[t] == jf)
        return hit

    kpos = own * MOBA_BLOCK + lax.broadcasted_iota(jnp.int32, (MOBA_BLOCK, n_cols), 0)
    qpos = n * Q_TILE + (lax.broadcasted_iota(jnp.int32, (MOBA_BLOCK, n_cols), 1) & (Q_TILE - 1))
    st = jnp.where(kpos <= qpos, scores(own), NEG)
    m = jnp.max(st, axis=0, keepdims=True)
    acc = _nn(vt_scr[own], jnp.exp2(st - m).astype(BF16))

    n_groups = (own + ATTN_GROUP - 1) // ATTN_GROUP

    def produce(grp):
        mg = jnp.full((1, n_cols), NEG, F32)
        for u in range(ATTN_GROUP):
            j = grp * ATTN_GROUP + u
            st = scores(jnp.minimum(j, nblk - 1))
            st_scr[grp % 2, u] = st
            mg = jnp.maximum(mg, jnp.where(chose(j), jnp.max(st, axis=0, keepdims=True), NEG))
        return mg

    if k_sel > 0:
        def body(grp, carry):
            m, acc, mg = carry
            m_new = jnp.maximum(m, mg)
            alpha = jnp.exp2(m - m_new)
            acc = alpha * acc
            for u in range(ATTN_GROUP):
                j = grp * ATTN_GROUP + u
                p = jnp.exp2(st_scr[grp % 2, u] - jnp.where(chose(j), m_new, -NEG))
                acc = acc + _nn(vt_scr[jnp.minimum(j, nblk - 1)], p.astype(BF16))
            mg_next = produce(grp + 1)
            return m_new, acc, mg_next
        _, acc, _ = lax.fori_loop(0, n_groups, body, (m, acc, produce(0)))

    o = acc[:ATTN_HEADDIM] / acc[ATTN_HEADDIM:ATTN_HEADDIM + 1]
    gg = _silu(g_ref[...])
    for r in range(Q_PER_KV):
        o_ref[:, LANES * r:LANES * (r + 1)] = (
            o[:, Q_TILE * r:Q_TILE * (r + 1)].T * gg[:, LANES * r:LANES * (r + 1)]).astype(o_ref.dtype)


def _attn_prompt(q, g, k, v, batch, seq):
    nq = seq // Q_TILE
    nblk = seq // MOBA_BLOCK
    k_sel = min(MOBA_TOPK, nblk - 1)
    qmap = lambda b, h, n: (b * nq + n, h)
    kvmap = lambda b, h, n: (b, h)
    width = Q_PER_KV * ATTN_HEADDIM
    return pl.pallas_call(
        functools.partial(_attn_prompt_kernel, k_sel=k_sel),
        grid=(batch, KV_HEADS, nq),
        in_specs=[
            pl.BlockSpec((Q_TILE, width), qmap),
            pl.BlockSpec((Q_TILE, width), qmap),
            pl.BlockSpec((seq, ATTN_HEADDIM), kvmap),
            pl.BlockSpec((seq, ATTN_HEADDIM), kvmap),
        ],
        out_specs=pl.BlockSpec((Q_TILE, width), qmap),
        out_shape=jax.ShapeDtypeStruct((batch * seq, ATTN_WIDTH), BF16),
        scratch_shapes=[
            pltpu.VMEM((nblk, MOBA_BLOCK, ATTN_HEADDIM), BF16),
            pltpu.VMEM((nblk, ATTN_HEADDIM + SUM_ROWS, MOBA_BLOCK), BF16),
            pltpu.VMEM((nblk, ATTN_HEADDIM), F32),
            pltpu.VMEM((2, ATTN_GROUP, MOBA_BLOCK, Q_PER_KV * Q_TILE), F32),
        ],
        compiler_params=pltpu.CompilerParams(
            dimension_semantics=("parallel", "parallel", "arbitrary"),
            vmem_limit_bytes=VMEM_LIMIT),
        name="moba_prompt",
    )(q, g, k, v)


def _out_kernel(ys_ref, ya_ref, x_ref, w1_ref, w2_ref, nw_ref, o_ref):
    out = x_ref[...] + _nn(ys_ref[...].astype(BF16), w1_ref[...])
    out = out + _nn(ya_ref[...].astype(BF16), w2_ref[...])
    o_ref[...] = _rmsnorm(out, nw_ref[...])


def _out_proj(ys, ya, x, w1, w2, nw, tm):
    t = x.shape[0]
    row = lambda i: (i, 0)
    const = lambda i: (0, 0)
    return pl.pallas_call(
        _out_kernel,
        grid=(t // tm,),
        in_specs=[
            pl.BlockSpec((tm, SSM_WIDTH), row),
            pl.BlockSpec((tm, ATTN_WIDTH), row),
            pl.BlockSpec((tm, D_MODEL), row),
            pl.BlockSpec((SSM_WIDTH, D_MODEL), const),
            pl.BlockSpec((ATTN_WIDTH, D_MODEL), const),
            pl.BlockSpec((1, D_MODEL), const),
        ],
        out_specs=pl.BlockSpec((tm, D_MODEL), row),
        out_shape=jax.ShapeDtypeStruct((t, D_MODEL), F32),
        compiler_params=pltpu.CompilerParams(
            dimension_semantics=("parallel",), vmem_limit_bytes=VMEM_LIMIT),
        name="out_proj",
    )(ys, ya, x, w1, w2, nw)


def _ssd_sample_kernel(xbc_ref, cst_ref, z_ref, dt_ref, h0_ref, cw_ref, cb_ref, dtb_ref, alog_ref,
                       dsk_ref, nw_ref, eh_ref, eg_ref,
                       y_ref, hnew_ref, cnew_ref, *, steps):
    tb = SAMPLE_TILE
    hist = CONV_WIDTH - 1
    rows = [cst_ref[:, j, :] for j in range(hist)] + [xbc_ref[:, t, :] for t in range(steps)]
    for j in range(hist):
        cnew_ref[:, j, :] = rows[steps + j]

    a_neg = -jnp.exp(alog_ref[...])
    u, dts, acs = [], [], []
    run = None
    for t in range(steps):
        conv = cb_ref[...]
        for i in range(CONV_WIDTH):
            conv = conv + rows[t + i] * cw_ref[i:i + 1, :]
        u.append(_silu(conv))
        dt = _softplus(dt_ref[:, t, :] + dtb_ref[...])
        dts.append(dt)
        run = dt * a_neg if run is None else run + dt * a_neg
        acs.append(run)
    uu = jnp.concatenate(u, axis=0)
    xs = uu[:, :SSM_WIDTH]
    bm = uu[:, SSM_WIDTH:SSM_WIDTH + SSM_GROUPS * SSM_STATE]
    cm = uu[:, SSM_WIDTH + SSM_GROUPS * SSM_STATE:]
    dt_all = jnp.concatenate(dts, axis=0)
    ac_all = jnp.concatenate(acs, axis=0)
    a_last = jnp.concatenate([acs[-1]] * steps, axis=0)

    eh = eh_ref[...]
    xdt = xs * _nn_exact_rhs(dt_all, eh)
    xw = xs * _nn_exact_rhs(dt_all * jnp.exp(a_last - ac_all), eh)
    ea = _nn_exact_rhs(jnp.exp(ac_all), eh)

    eg = eg_ref[...]
    ydiag = []
    for t in range(steps):
        acc = None
        for j in range(t + 1):
            cbx = _nn_exact_rhs(cm[tb * t:tb * (t + 1)] * bm[tb * j:tb * (j + 1)], eg)
            lmx = _nn_exact_rhs(jnp.exp(acs[t] - acs[j]), eh)
            term = cbx * lmx * xdt[tb * j:tb * (j + 1)]
            acc = term if acc is None else acc + term
        ydiag.append(acc)
    y = jnp.concatenate(ydiag, axis=0) + dsk_ref[...] * xs

    rowb = lax.broadcasted_iota(jnp.int32, (steps * tb, 1), 0) % tb
    cmb = cm.astype(BF16)
    gw = HEADS_PER_GROUP * SSM_HEADDIM
    yoff = [None] * SSM_GROUPS
    for b in range(tb):
        mine = rowb == b
        for g in range(SSM_GROUPS):
            h0g = h0_ref[b, HEADS_PER_GROUP * g:HEADS_PER_GROUP * (g + 1)].reshape(gw, SSM_STATE)
            part = jnp.where(mine, _nt(cmb[:, SSM_STATE * g:SSM_STATE * (g + 1)], h0g.astype(BF16)), 0.0)
            yoff[g] = part if yoff[g] is None else yoff[g] + part
            xwb = jnp.where(mine, xw[:, gw * g:gw * (g + 1)], 0.0)
            xh, xm, _ = _split3(xwb)
            bh, bmid, _ = _split3(bm[:, SSM_STATE * g:SSM_STATE * (g + 1)])
            st = _tn(xh, bh) + _tn(xh, bmid) + _tn(xm, bh)
            for r in range(HEADS_PER_GROUP):
                h = HEADS_PER_GROUP * g + r
                dec = jnp.exp(acs[-1][b:b + 1, h:h + 1])
                hnew_ref[b, h] = (h0g[SSM_HEADDIM * r:SSM_HEADDIM * (r + 1)] * dec
                                  + st[SSM_HEADDIM * r:SSM_HEADDIM * (r + 1)])
    y = y + jnp.concatenate(yoff, axis=1) * ea

    for t in range(steps):
        yt = y[tb * t:tb * (t + 1)] * _silu(z_ref[:, t, :])
        y_ref[:, t, :] = _rmsnorm(yt, nw_ref[...])


def _ssd_sample(xbc, cst, z, dt, h0, conv_w, conv_b, dtb, alog, dsk, nw, eh, eg):
    nb, steps = xbc.shape[0], xbc.shape[1]
    tb = SAMPLE_TILE
    c2 = lambda i: (0, 0)
    b3 = lambda i: (i, 0, 0)
    b4 = lambda i: (i, 0, 0, 0)
    hist = CONV_WIDTH - 1
    return pl.pallas_call(
        functools.partial(_ssd_sample_kernel, steps=steps),
        grid=(nb // tb,),
        in_specs=[
            pl.BlockSpec((tb, steps, XBC_WIDTH), b3),
            pl.BlockSpec((tb, hist, XBC_WIDTH), b3),
            pl.BlockSpec((tb, steps, SSM_WIDTH), b3),
            pl.BlockSpec((tb, steps, DT_PAD), b3),
            pl.BlockSpec((tb, SSM_HEADS, SSM_HEADDIM, SSM_STATE), b4),
            pl.BlockSpec((CONV_WIDTH, XBC_WIDTH), c2),
            pl.BlockSpec((1, XBC_WIDTH), c2),
            pl.BlockSpec((1, DT_PAD), c2),
            pl.BlockSpec((1, DT_PAD), c2),
            pl.BlockSpec((1, SSM_WIDTH), c2),
            pl.BlockSpec((1, SSM_WIDTH), c2),
            pl.BlockSpec((DT_PAD, SSM_WIDTH), c2),
            pl.BlockSpec((SSM_GROUPS * SSM_STATE, SSM_WIDTH), c2),
        ],
        out_specs=[
            pl.BlockSpec((tb, steps, SSM_WIDTH), b3),
            pl.BlockSpec((tb, SSM_HEADS, SSM_HEADDIM, SSM_STATE), b4),
            pl.BlockSpec((tb, hist, XBC_WIDTH), b3),
        ],
        out_shape=[
            jax.ShapeDtypeStruct((nb, steps, SSM_WIDTH), F32),
            jax.ShapeDtypeStruct((nb, SSM_HEADS, SSM_HEADDIM, SSM_STATE), F32),
            jax.ShapeDtypeStruct((nb, hist, XBC_WIDTH), F32),
        ],
        compiler_params=pltpu.CompilerParams(
            dimension_semantics=("parallel",), vmem_limit_bytes=VMEM_LIMIT),
        name="ssd_sample",
    )(xbc, cst, z, dt, h0, conv_w, conv_b, dtb, alog, dsk, nw, eh, eg)


def _attn_sample_kernel(pt_ref, q_ref, g_ref, kn_ref, vn_ref, ck_ref, cv_ref, o_ref,
                        kbuf, vbuf, km_scr, w_scr, idx_v, idx_s, o_scr, sems, *, n_pages, steps):
    b = pl.program_id(0)
    slot = b % 2
    n_full = n_pages // PAGES_PER_BLOCK
    page_rows = PAGE_SIZE * KV_HEADS
    n_cols = ATTN_HEADS * steps
    per_kv = Q_PER_KV * steps

    def k_copy(seq, sl, p):
        return pltpu.make_async_copy(ck_ref.at[pt_ref[seq * n_pages + p]], kbuf.at[sl, p], sems.at[sl])

    def v_copy(seq, p):
        return pltpu.make_async_copy(cv_ref.at[pt_ref[seq * n_pages + p]], vbuf.at[p], sems.at[2])

    @pl.when(b == 0)
    def _():
        for p in range(n_pages):
            k_copy(0, 0, p).start(priority=p % 2)
        for p in range(n_pages):
            v_copy(0, p).start(priority=p % 2)

    for p in range(n_pages):
        k_copy(b, slot, p).wait()

    km_scr[...] = jnp.zeros_like(km_scr)
    for blk in range(n_full):
        tot = None
        for pp in range(PAGES_PER_BLOCK):
            page = kbuf[slot, PAGES_PER_BLOCK * blk + pp]
            part = page.reshape(page_rows // 8, 8, LANES).sum(axis=0)
            tot = part if tot is None else tot + part
        km_scr[KV_HEADS * blk:KV_HEADS * (blk + 1), :] = (
            (tot[:KV_HEADS] + tot[KV_HEADS:]) * (1.0 / MOBA_BLOCK))

    qf = q_ref[0]
    q_all = jnp.concatenate([qf[:, LANES * h:LANES * (h + 1)] for h in range(ATTN_HEADS)], axis=0)
    q_pad = jnp.concatenate([q_all, jnp.zeros((LANES - n_cols, LANES), F32)], axis=0)

    gate = _nt_x3(q_pad, km_scr[...])
    lane = lax.broadcasted_iota(jnp.int32, gate.shape, 1)
    rowc = lax.broadcasted_iota(jnp.int32, gate.shape, 0)
    ok = jnp.logical_and(lane % KV_HEADS == rowc // per_kv, lane < KV_HEADS * n_full)
    gate = jnp.where(ok, gate, -jnp.inf)
    lanef = lane.astype(F32)
    chosen = jnp.zeros(gate.shape, jnp.int32)
    for t in range(MOBA_TOPK):
        m = jnp.max(gate, axis=-1, keepdims=True)
        idx = jnp.min(jnp.where(gate == m, lanef, float(4 * LANES)), axis=-1, keepdims=True)
        chosen = jnp.where(lane == t, idx.astype(jnp.int32) // KV_HEADS, chosen)
        gate = jnp.where(lanef == idx, -jnp.inf, gate)
    idx_v[...] = chosen
    to_smem = pltpu.make_async_copy(idx_v, idx_s, sems.at[3])
    to_smem.start()

    q_t = q_pad.T
    for c in range(n_cols):
        w_scr[c] = jnp.broadcast_to(q_t[:, c:c + 1], (ATTN_HEADDIM, LANES)).astype(BF16)

    to_smem.wait()
    for p in range(n_pages):
        v_copy(b, p).wait()

    has_next = b + 1 < pl.num_programs(0)
    n_iters = KV_HEADS * (per_kv // 2)
    pages_per_iter = -(-n_pages // n_iters)

    def request_next_k(it):
        if n_pages % n_iters == 0:
            @pl.when(has_next)
            def _():
                for q in range(pages_per_iter):
                    k_copy(b + 1, 1 - slot, it * pages_per_iter + q).start(priority=q % 2)
        else:
            for q in range(pages_per_iter):
                page = it * pages_per_iter + q

                @pl.when(jnp.logical_and(has_next, page < n_pages))
                def _(page=page, q=q):
                    k_copy(b + 1, 1 - slot, page).start(priority=q % 2)

    sub8 = lax.broadcasted_iota(jnp.int32, (8, LANES), 0)

    def k_tile(blk, g):
        return jnp.concatenate(
            [kbuf[slot, PAGES_PER_BLOCK * blk + pp, pl.ds(g, PAGE_SIZE, stride=KV_HEADS), :]
             for pp in range(PAGES_PER_BLOCK)], axis=0)

    def v_tile(blk, g):
        return jnp.concatenate(
            [vbuf[PAGES_PER_BLOCK * blk + pp, pl.ds(g, PAGE_SIZE, stride=KV_HEADS), :]
             for pp in range(PAGES_PER_BLOCK)], axis=0)

    def column(c, i, g, knb, vn):
        w = w_scr[c]
        s_own = jnp.where(sub8 <= i % steps, _nn(knb, w) * SOFTMAX_C, NEG)
        mx = jnp.max(s_own, axis=0, keepdims=True)
        blks = [idx_s[c, t] for t in range(MOBA_TOPK)]
        scores = []
        for blk in blks:
            s = _nn(k_tile(blk, g).astype(BF16), w) * SOFTMAX_C
            scores.append(s)
            mx = jnp.maximum(mx, jnp.max(s, axis=0, keepdims=True))
        p = jnp.exp2(s_own - mx)
        l = jnp.sum(p, axis=0, keepdims=True)
        acc = jnp.sum(p * vn, axis=0, keepdims=True)
        for blk, s in zip(blks, scores):
            p = jnp.exp2(s - mx)
            l = l + jnp.sum(p, axis=0, keepdims=True)
            acc = acc + jnp.sum(p * v_tile(blk, g), axis=0, keepdims=True)
        return acc / l

    for g in range(KV_HEADS):
        pad = jnp.zeros((8 - steps, LANES), F32)
        kn = jnp.concatenate([kn_ref[0][:, LANES * g:LANES * (g + 1)], pad], axis=0)
        vn = jnp.concatenate([vn_ref[0][:, LANES * g:LANES * (g + 1)], pad], axis=0)
        knb = kn.astype(BF16)

        def pair_body(ip, _, g=g, knb=knb, vn=vn):
            request_next_k(g * (per_kv // 2) + ip)
            for u in range(2):
                i = 2 * ip + u
                c = g * per_kv + i
                o_scr[pl.ds(c, 1), :] = column(c, i, g, knb, vn)
            return 0

        lax.fori_loop(0, per_kv // 2, pair_body, 0)

    @pl.when(b + 1 < pl.num_programs(0))
    def _():
        for p in range(n_pages):
            v_copy(b + 1, p).start(priority=p % 2)

    gg = _silu(g_ref[0])
    for h in range(ATTN_HEADS):
        o_ref[0, :, LANES * h:LANES * (h + 1)] = (
            o_scr[steps * h:steps * (h + 1), :] * gg[:, LANES * h:LANES * (h + 1)])


def _attn_sample(page_table, q, g, k_new, v_new, cache_k, cache_v):
    nb, steps = q.shape[0], q.shape[1]
    n_pages = page_table.shape[1]
    n_pool = cache_k.shape[0]
    page_rows = PAGE_SIZE * KV_HEADS
    ck = cache_k.reshape(n_pool, page_rows, ATTN_HEADDIM)
    cv = cache_v.reshape(n_pool, page_rows, ATTN_HEADDIM)
    b3 = lambda i, pt: (i, 0, 0)
    grid_spec = pltpu.PrefetchScalarGridSpec(
        num_scalar_prefetch=1,
        grid=(nb,),
        in_specs=[
            pl.BlockSpec((1, steps, ATTN_WIDTH), b3),
            pl.BlockSpec((1, steps, ATTN_WIDTH), b3),
            pl.BlockSpec((1, steps, KV_WIDTH), b3),
            pl.BlockSpec((1, steps, KV_WIDTH), b3),
            pl.BlockSpec(memory_space=pl.ANY),
            pl.BlockSpec(memory_space=pl.ANY),
        ],
        out_specs=pl.BlockSpec((1, steps, ATTN_WIDTH), b3),
        scratch_shapes=[
            pltpu.VMEM((2, n_pages, page_rows, ATTN_HEADDIM), F32),
            pltpu.VMEM((n_pages, page_rows, ATTN_HEADDIM), F32),
            pltpu.VMEM((LANES, ATTN_HEADDIM), F32),
            pltpu.VMEM((ATTN_HEADS * steps, ATTN_HEADDIM, LANES), BF16),
            pltpu.VMEM((LANES, LANES), jnp.int32),
            pltpu.SMEM((LANES, LANES), jnp.int32),
            pltpu.VMEM((ATTN_HEADS * steps, ATTN_HEADDIM), F32),
            pltpu.SemaphoreType.DMA((4,)),
        ],
    )
    return pl.pallas_call(
        functools.partial(_attn_sample_kernel, n_pages=n_pages, steps=steps),
        grid_spec=grid_spec,
        out_shape=jax.ShapeDtypeStruct((nb, steps, ATTN_WIDTH), F32),
        compiler_params=pltpu.CompilerParams(
            dimension_semantics=("arbitrary",), vmem_limit_bytes=VMEM_LIMIT),
        name="moba_sample",
    )(page_table.reshape(-1), q, g, k_new, v_new, ck, cv)


def _rope_tables(pos):
    inv = 1.0 / (ROPE_THETA ** (jnp.arange(0, ROT_DIM, 2, dtype=F32) / ROT_DIM))
    ang = pos.astype(F32)[:, None] * inv[None, :]
    cos, sin = jnp.cos(ang), jnp.sin(ang)
    n = pos.shape[0]
    rest = LANES - ROT_DIM
    cosf = jnp.concatenate([cos, cos, jnp.ones((n, rest), F32)], axis=1)
    sinf = jnp.concatenate([-sin, sin, jnp.zeros((n, rest), F32)], axis=1)
    return cosf, sinf


def _row_tile(t, cap):
    tm = cap
    while t % tm:
        tm //= 2
    return tm


def kernel(x_prompt, x_sample, cache_k, cache_v, state_ssm, state_conv, page_table,
           norm_w, w_in, conv_w, conv_b, dt_bias, a_log, d_skip, ssm_norm_w, w_out, final_norm_w):
    depth = w_in.shape[0]
    assert depth == 1
    bp, seq, _ = x_prompt.shape
    nb, steps, _ = x_sample.shape
    n_pages = page_table.shape[1]
    past = n_pages * PAGE_SIZE
    assert seq % MOBA_BLOCK == 0 and past % MOBA_BLOCK == 0 and steps <= 8
    assert n_pages // PAGES_PER_BLOCK >= MOBA_TOPK and nb % SAMPLE_TILE == 0
    assert KV_HEADS * (n_pages // PAGES_PER_BLOCK) <= LANES

    w = w_in[0]
    c_dt = SSM_WIDTH + XBC_WIDTH
    wa = w[:, :c_dt].astype(BF16)
    wb = w[:, c_dt + SSM_HEADS:].astype(BF16)
    wdt = jnp.pad(w[:, c_dt:c_dt + SSM_HEADS], ((0, 0), (0, DT_PAD - SSM_HEADS))).astype(BF16)
    nw = norm_w[0].reshape(1, D_MODEL)
    pad_h = lambda v: jnp.concatenate([v, jnp.zeros((DT_PAD - SSM_HEADS,), F32)]).reshape(1, DT_PAD)
    dtb, alog = pad_h(dt_bias[0]), pad_h(a_log[0])
    dsk = jnp.repeat(d_skip[0], SSM_HEADDIM).reshape(1, SSM_WIDTH)
    snw = ssm_norm_w[0].reshape(1, SSM_WIDTH)
    cb = conv_b[0].reshape(1, XBC_WIDTH)
    w1 = w_out[0][:SSM_WIDTH].astype(BF16)
    w2 = w_out[0][SSM_WIDTH:].astype(BF16)
    fnw = final_norm_w.reshape(1, D_MODEL)

    heads = jnp.arange(DT_PAD)[:, None]
    lanes = jnp.arange(SSM_WIDTH)[None, :]
    eh = (lanes // SSM_HEADDIM == heads).astype(BF16)
    rows_g = jnp.arange(SSM_GROUPS * SSM_STATE)[:, None] // SSM_STATE
    eg = (lanes // (HEADS_PER_GROUP * SSM_HEADDIM) == rows_g).astype(BF16)

    xp = x_prompt.reshape(bp * seq, D_MODEL)
    tm_p = _row_tile(seq, 256)
    cos_p, sin_p = _rope_tables(jnp.arange(seq))
    z, xbc, q, k, v, g, dt, k_rows, v_rows = _proj(xp, nw, wa, wb, wdt, cos_p, sin_p, tm_p)
    y_ssm, h_p, c_p = _ssd_prompt(xbc, z, dt, conv_w[0], cb, dtb, alog, dsk, snw, eh, bp, seq)
    y_attn = _attn_prompt(q, g, k, v, bp, seq)
    y_p = _out_proj(y_ssm, y_attn, xp, w1, w2, fnw, _row_tile(bp * seq, 512))

    ts = nb * steps
    xs = x_sample.reshape(ts, D_MODEL)
    tm_s = _row_tile(ts, 256)
    cos_s, sin_s = _rope_tables(past + jnp.arange(steps))
    cos_s, sin_s = jnp.tile(cos_s, (tm_s // steps, 1)), jnp.tile(sin_s, (tm_s // steps, 1))
    zs, xbcs, qs, ks, vs, gs, dts, ks_rows, vs_rows = _proj(xs, nw, wa, wb, wdt, cos_s, sin_s, tm_s)
    r3 = lambda a: a.reshape(nb, steps, a.shape[-1])
    ys_ssm, h_s, c_s = _ssd_sample(r3(xbcs), state_conv[0], r3(zs), r3(dts), state_ssm[0],
                                   conv_w[0], cb, dtb, alog, dsk, snw, eh, eg)
    ys_attn = _attn_sample(page_table, r3(qs), r3(gs), r3(ks), r3(vs), cache_k[0], cache_v[0])
    y_s = _out_proj(ys_ssm.reshape(ts, SSM_WIDTH), ys_attn.reshape(ts, ATTN_WIDTH), xs, w1, w2, fnw,
                    _row_tile(ts, 512))

    kv5 = lambda a, n, s: a.reshape(1, n, s, KV_HEADS, ATTN_HEADDIM)
    return (y_p.reshape(bp, seq, D_MODEL), y_s.reshape(nb, steps, D_MODEL),
            kv5(k_rows, bp, seq), kv5(v_rows, bp, seq), kv5(ks_rows, nb, steps), kv5(vs_rows, nb, steps),
            h_p[None], c_p[None], h_s[None], c_s[None])
```
